```python
import functools
import jax, jax.numpy as jnp
from jax import lax
import numpy as np

D_MODEL = 1024
BATCH = 8
SEQ = 4096
DEPTH = 2
DEC_BATCH = 32
DEC_SEQ = 4
PAST_LEN = 16384
PAGE_SIZE = 128

SGU_WIDTH = 256
SGU_GROUPS = 4
SGU_GROUP_DIM = SGU_WIDTH // SGU_GROUPS
CHUNK = 128
N_HEADS = 8
HEAD_DIM = 64
ATT_WIDTH = N_HEADS * HEAD_DIM
Q_BLOCK = 128
FORGET_BIAS_MEAN = 4.0
CONV_CH = 256
CONV_W = 31
D_FF = 2816
FFN_CONV_W = 3
N_BRANCH = 3
EPS = 1e-6
NEG_INF = -1e30

OFF_SGU = 0
OFF_Q = OFF_SGU + 2 * SGU_WIDTH
OFF_K = OFF_Q + ATT_WIDTH
OFF_V = OFF_K + ATT_WIDTH
OFF_F = OFF_V + ATT_WIDTH
OFF_GLU = OFF_F + N_HEADS
OFF_GATE = OFF_GLU + 2 * CONV_CH
IN_COLS = OFF_GATE + N_BRANCH * D_MODEL

kernel_name = 'hybrid_sgu_fox_conformer_decode_step'


def rms_norm(x, g):
    xf = x.astype(jnp.float32)
    y = xf * lax.rsqrt(jnp.mean(xf * xf, axis=-1, keepdims=True) + EPS)
    return (y * g.astype(jnp.float32)).astype(x.dtype)


def layer_norm(x, g, b):
    xf = x.astype(jnp.float32)
    mu = jnp.mean(xf, axis=-1, keepdims=True)
    var = jnp.mean(jnp.square(xf - mu), axis=-1, keepdims=True)
    y = (xf - mu) * lax.rsqrt(var + EPS)
    return (y * g.astype(jnp.float32) + b.astype(jnp.float32)).astype(x.dtype)


def causal_dwconv(x_ext, w, b):
    c = x_ext.shape[-1]
    y = lax.conv_general_dilated(x_ext, w[:, None, :].astype(x_ext.dtype), window_strides=(1,), padding='VALID',
                                 dimension_numbers=('NWC', 'WIO', 'NWC'), feature_group_count=c)
    return y + b


def chunk_spatial_gate(u, v, w_s, b_s):
    bsz, t, _ = v.shape
    n_chunks = -(-t // CHUNK)
    tp = n_chunks * CHUNK
    vp = jnp.pad(v, ((0, 0), (0, tp - t), (0, 0))).reshape(bsz, n_chunks, CHUNK, SGU_GROUPS, SGU_GROUP_DIM)
    w_causal = jnp.where(jnp.tril(jnp.ones((CHUNK, CHUNK), dtype=bool)), w_s, 0.0)
    mixed = jnp.einsum('gts,bcsgd->bctgd', w_causal, vp) + b_s.T[:, :, None]
    mixed = mixed.reshape(bsz, tp, SGU_WIDTH)[:, :t]
    return u * mixed


def fox_prompt(q, k, v, log_f):
    bsz, s = q.shape[0], q.shape[1]
    nb = s // Q_BLOCK
    c = jnp.cumsum(log_f, axis=1)
    c_k = jnp.transpose(c, (0, 2, 1))[:, :, None, :]
    kpos = jnp.arange(s)
    qb = q.reshape(bsz, nb, Q_BLOCK, N_HEADS, HEAD_DIM).transpose(1, 0, 2, 3, 4)
    cb = c.reshape(bsz, nb, Q_BLOCK, N_HEADS).transpose(1, 0, 3, 2)
    scale = HEAD_DIM ** -0.5

    def block(args):
        q_blk, c_blk, i = args
        logits = jnp.einsum('bqhd,bkhd->bhqk', q_blk, k, preferred_element_type=jnp.float32) * scale
        logits = logits + (c_blk[..., None] - c_k)
        qpos = i * Q_BLOCK + jnp.arange(Q_BLOCK)
        logits = jnp.where(kpos[None, :] <= qpos[:, None], logits, NEG_INF)
        p = jax.nn.softmax(logits, axis=-1)
        return jnp.einsum('bhqk,bkhd->bqhd', p.astype(v.dtype), v)

    out = lax.map(block, (qb, cb, jnp.arange(nb)))
    return out.transpose(1, 0, 2, 3, 4).reshape(bsz, s, ATT_WIDTH)


def fox_sample(q, k, v, log_f, k_past, v_past, lf_past):
    bsz, t = q.shape[0], q.shape[1]
    lf_past = lf_past.astype(jnp.float32)
    c_new = jnp.transpose(jnp.cumsum(log_f, axis=1), (0, 2, 1))
    r_past = lax.cumsum(lf_past, axis=1, reverse=True) - lf_past
    r_past = jnp.transpose(r_past, (0, 2, 1))
    bias_past = c_new[..., None] + r_past[:, :, None, :]
    causal = jnp.tril(jnp.ones((t, t), dtype=bool))
    bias_new = jnp.where(causal, c_new[..., None] - c_new[:, :, None, :], NEG_INF)
    scale = HEAD_DIM ** -0.5
    lp = jnp.einsum('bqhd,bkhd->bhqk', q, k_past, preferred_element_type=jnp.float32) * scale + bias_past
    ln = jnp.einsum('bqhd,bkhd->bhqk', q, k, preferred_element_type=jnp.float32) * scale + bias_new
    p = jax.nn.softmax(jnp.concatenate([lp, ln], axis=-1), axis=-1).astype(v.dtype)
    n_past = k_past.shape[1]
    out = (jnp.einsum('bhqk,bkhd->bqhd', p[..., :n_past], v_past)
           + jnp.einsum('bhqk,bkhd->bqhd', p[..., n_past:], v))
    return out.reshape(bsz, t, ATT_WIDTH)


def trunk_layer(x, lp, attend, conv_hist, ffn_hist):
    (norm1_g, w_in, b_forget, sgu_ln_g, sgu_ln_b, sgu_w, sgu_b, w_proj_a, w_proj_b,
     conv_w, conv_b, conv_ln_g, conv_ln_b, w_proj_c, w_out, norm2_g, w_up,
     ffn_conv_w, ffn_conv_b, w_down) = lp
    bsz, t, _ = x.shape
    h = rms_norm(x, norm1_g)
    z = h @ w_in
    a = jax.nn.gelu(z[..., OFF_SGU:OFF_Q])
    a_u = a[..., :SGU_WIDTH]
    a_v = layer_norm(a[..., SGU_WIDTH:], sgu_ln_g, sgu_ln_b)
    y_a = chunk_spatial_gate(a_u, a_v, sgu_w, sgu_b) @ w_proj_a
    q = z[..., OFF_Q:OFF_K].reshape(bsz, t, N_HEADS, HEAD_DIM)
    k = z[..., OFF_K:OFF_V].reshape(bsz, t, N_HEADS, HEAD_DIM)
    v = z[..., OFF_V:OFF_F].reshape(bsz, t, N_HEADS, HEAD_DIM)
    log_f = jax.nn.log_sigmoid((z[..., OFF_F:OFF_GLU] + b_forget).astype(jnp.float32))
    y_b = attend(q, k, v, log_f) @ w_proj_b
    glu = z[..., OFF_GLU:OFF_GLU + CONV_CH] * jax.nn.sigmoid(z[..., OFF_GLU + CONV_CH:OFF_GATE])
    glu_ext = jnp.concatenate([conv_hist.astype(glu.dtype), glu], axis=1)
    cc = causal_dwconv(glu_ext, conv_w, conv_b)
    y_c = jax.nn.silu(layer_norm(cc, conv_ln_g, conv_ln_b)) @ w_proj_c
    gates = jax.nn.sigmoid(z[..., OFF_GATE:].reshape(bsz, t, N_BRANCH, D_MODEL))
    merged = gates[..., 0, :] * y_a + gates[..., 1, :] * y_b + gates[..., 2, :] * y_c
    x = x + merged @ w_out
    h2 = rms_norm(x, norm2_g)
    up = h2 @ w_up
    up_ext = jnp.concatenate([ffn_hist.astype(up.dtype), up], axis=1)
    upc = causal_dwconv(up_ext, ffn_conv_w, ffn_conv_b)
    act = jax.nn.gelu(upc[..., :D_FF]) * upc[..., D_FF:]
    x = x + act @ w_down
    new_state = (k, v, log_f, a_v, glu_ext[:, -(CONV_W - 1):], up_ext[:, -(FFN_CONV_W - 1):])
    return x, new_state


def setup_inputs(seed: int = 0) -> dict:
    key = jax.random.key(seed)
    ks = jax.random.split(key, 32)
    n_pages = PAST_LEN // PAGE_SIZE
    n_used = DEC_BATCH * n_pages
    n_phys = n_used + max(1, n_used // 4)
    f32 = jnp.float32

    def nrm(k, shape, scale=1.0):
        return jax.random.normal(k, shape, f32) * scale

    page_table = jax.random.permutation(ks[7], n_phys)[:n_used].reshape(DEC_BATCH, n_pages).astype(jnp.int32)
    return {
        'x_prompt': nrm(ks[0], (BATCH, SEQ, D_MODEL)),
        'x_sample': nrm(ks[1], (DEC_BATCH, DEC_SEQ, D_MODEL)),
        'cache_k': nrm(ks[2], (n_phys, DEPTH, PAGE_SIZE, N_HEADS, HEAD_DIM)),
        'cache_v': nrm(ks[3], (n_phys, DEPTH, PAGE_SIZE, N_HEADS, HEAD_DIM)),
        'cache_logf': jax.nn.log_sigmoid(FORGET_BIAS_MEAN + nrm(ks[4], (n_phys, DEPTH, PAGE_SIZE, N_HEADS))),
        'state_conv': nrm(ks[5], (DEC_BATCH, DEPTH, CONV_W - 1, CONV_CH), 0.5),
        'state_ffn': nrm(ks[6], (DEC_BATCH, DEPTH, FFN_CONV_W - 1, 2 * D_FF)),
        'page_table': page_table,
        'norm1_g': 1.0 + nrm(ks[8], (DEPTH, D_MODEL), 0.1),
        'w_in': nrm(ks[9], (DEPTH, D_MODEL, IN_COLS), D_MODEL ** -0.5),
        'b_forget': FORGET_BIAS_MEAN + nrm(ks[10], (DEPTH, N_HEADS), 0.5),
        'sgu_ln_g': 1.0 + nrm(ks[11], (DEPTH, SGU_WIDTH), 0.1),
        'sgu_ln_b': nrm(ks[12], (DEPTH, SGU_WIDTH), 0.02),
        'sgu_w': nrm(ks[13], (DEPTH, SGU_GROUPS, CHUNK, CHUNK), CHUNK ** -0.5),
        'sgu_b': 1.0 + nrm(ks[14], (DEPTH, SGU_GROUPS, CHUNK), 0.1),
        'w_proj_a': nrm(ks[15], (DEPTH, SGU_WIDTH, D_MODEL), SGU_WIDTH ** -0.5),
        'w_proj_b': nrm(ks[16], (DEPTH, ATT_WIDTH, D_MODEL), ATT_WIDTH ** -0.5),
        'conv_w': nrm(ks[17], (DEPTH, CONV_W, CONV_CH), CONV_W ** -0.5),
        'conv_b': nrm(ks[18], (DEPTH, CONV_CH), 0.02),
        'conv_ln_g': 1.0 + nrm(ks[19], (DEPTH, CONV_CH), 0.1),
        'conv_ln_b': nrm(ks[20], (DEPTH, CONV_CH), 0.02),
        'w_proj_c': nrm(ks[21], (DEPTH, CONV_CH, D_MODEL), CONV_CH ** -0.5),
        'w_out': nrm(ks[22], (DEPTH, D_MODEL, D_MODEL), D_MODEL ** -0.5),
        'norm2_g': 1.0 + nrm(ks[23], (DEPTH, D_MODEL), 0.1),
        'w_up': nrm(ks[24], (DEPTH, D_MODEL, 2 * D_FF), D_MODEL ** -0.5),
        'ffn_conv_w': nrm(ks[25], (DEPTH, FFN_CONV_W, 2 * D_FF), FFN_CONV_W ** -0.5),
        'ffn_conv_b': nrm(ks[26], (DEPTH, 2 * D_FF), 0.02),
        'w_down': nrm(ks[27], (DEPTH, D_FF, D_MODEL), D_FF ** -0.5),
        'norm_f_g': 1.0 + nrm(ks[28], (D_MODEL,), 0.1),
    }


def reference(x_prompt, x_sample, cache_k, cache_v, cache_logf, state_conv, state_ffn, page_table,
              norm1_g, w_in, b_forget, sgu_ln_g, sgu_ln_b, sgu_w, sgu_b, w_proj_a, w_proj_b,
              conv_w, conv_b, conv_ln_g, conv_ln_b, w_proj_c, w_out, norm2_g, w_up,
              ffn_conv_w, ffn_conv_b, w_down, norm_f_g):
    layer_weights = (norm1_g, w_in, b_forget, sgu_ln_g, sgu_ln_b, sgu_w, sgu_b, w_proj_a, w_proj_b,
                     conv_w, conv_b, conv_ln_g, conv_ln_b, w_proj_c, w_out, norm2_g, w_up,
                     ffn_conv_w, ffn_conv_b, w_down)
    n_prompt = x_prompt.shape[0]
    n_seq = x_sample.shape[0]
    past_len = page_table.shape[1] * PAGE_SIZE
    xp, xs = x_prompt, x_sample
    kp, vp, fp, cp, ffp = [], [], [], [], []
    ksl, vsl, fsl, avsl, csl, ffsl = [], [], [], [], [], []
    for l in range(DEPTH):
        lp = tuple(w[l] for w in layer_weights)
        conv0 = jnp.zeros((n_prompt, CONV_W - 1, CONV_CH), xp.dtype)
        ffn0 = jnp.zeros((n_prompt, FFN_CONV_W - 1, 2 * D_FF), xp.dtype)
        xp, (k_, v_, f_, _, c_, ff_) = trunk_layer(xp, lp, fox_prompt, conv0, ffn0)
        kp.append(k_); vp.append(v_); fp.append(f_); cp.append(c_); ffp.append(ff_)
        k_past = cache_k[page_table, l].reshape(n_seq, past_len, N_HEADS, HEAD_DIM)
        v_past = cache_v[page_table, l].reshape(n_seq, past_len, N_HEADS, HEAD_DIM)
        lf_past = cache_logf[page_table, l].reshape(n_seq, past_len, N_HEADS)
        attend_s = functools.partial(fox_sample, k_past=k_past, v_past=v_past, lf_past=lf_past)
        xs, (k_, v_, f_, av_, c_, ff_) = trunk_layer(xs, lp, attend_s, state_conv[:, l], state_ffn[:, l])
        ksl.append(k_); vsl.append(v_); fsl.append(f_); avsl.append(av_); csl.append(c_); ffsl.append(ff_)
    y_prompt = rms_norm(xp, norm_f_g)
    y_sample = rms_norm(xs, norm_f_g)
    return (y_prompt, y_sample,
            jnp.stack(kp, axis=1), jnp.stack(vp, axis=1), jnp.stack(fp, axis=1),
            jnp.stack(cp, axis=1), jnp.stack(ffp, axis=1),
            jnp.stack(ksl, axis=1), jnp.stack(vsl, axis=1), jnp.stack(fsl, axis=1),
            jnp.stack(avsl, axis=1), jnp.stack(csl, axis=1), jnp.stack(ffsl, axis=1))
```

```python
import functools

import jax
import jax.numpy as jnp
from jax import lax
from jax.experimental import pallas as pl
from jax.experimental.pallas import tpu as pltpu

F32 = jnp.float32
BF16 = jnp.bfloat16

D_MODEL = 1024
SGU_WIDTH = 256
SGU_GROUPS = 4
CHUNK = 128
N_HEADS = 8
HEAD_DIM = 64
ATT_WIDTH = N_HEADS * HEAD_DIM
CONV_CH = 256
CONV_W = 31
D_FF = 2816
FFN_CONV_W = 3
N_BRANCH = 3
PAGE_SIZE = 128
EPS = 1e-6
NEG_INF = -1e30

OFF_F = 2 * SGU_WIDTH + 3 * ATT_WIDTH
OFF_GLU = OFF_F + N_HEADS
IN_COLS = OFF_GLU + 2 * CONV_CH + N_BRANCH * D_MODEL

LANES = 128
SUBLANES = 8
R_Q = 2 * SGU_WIDTH
R_K = R_Q + ATT_WIDTH
R_V = R_K + ATT_WIDTH
R_GLU = R_V + ATT_WIDTH
R_GATE = R_GLU + 2 * CONV_CH
R_F = R_GATE + N_BRANCH * D_MODEL
R_COLS = R_F + LANES

VMEM_LIMIT_BYTES = 56 * 1024 * 1024
PAGES_PER_STEP = 8


def _dot(a, b):
    return jnp.dot(a, b, preferred_element_type=F32)


def _dot_nt(a, b):
    return lax.dot_general(a, b, (((1,), (1,)), ((), ())), preferred_element_type=F32)


def _split3(x):
    hi = x.astype(BF16)
    r1 = x - hi.astype(F32)
    mid = r1.astype(BF16)
    lo = (r1 - mid.astype(F32)).astype(BF16)
    return hi, mid, lo


def _dot01(m01, x):
    hi, mid, lo = _split3(x)
    return _dot(m01, hi) + _dot(m01, mid) + _dot(m01, lo)


def _rms(x, g):
    return x * lax.rsqrt(jnp.mean(x * x, axis=-1, keepdims=True) + EPS) * g


def _layer_norm(x, g, b):
    mu = jnp.mean(x, axis=-1, keepdims=True)
    xc = x - mu
    var = jnp.mean(xc * xc, axis=-1, keepdims=True)
    return xc * lax.rsqrt(var + EPS) * g + b


def _log_sigmoid(x):
    return jnp.minimum(x, 0.0) - jnp.log1p(jnp.exp(-jnp.abs(x)))


def _const_spec(shape):
    nd = len(shape)
    return pl.BlockSpec(shape, lambda *_: (0,) * nd, pipeline_mode=pl.Buffered(1))


def _params(n_axes):
    return pltpu.CompilerParams(dimension_semantics=("arbitrary",) * n_axes,
                                vmem_limit_bytes=VMEM_LIMIT_BYTES)


def _inproj_kernel(x_ref, g_ref, w_ref, bf_ref, lng_ref, lnb_ref, tri_ref,
                   au_ref, av_ref, q_ref, k_ref, v_ref, kb_ref, vb_ref, lf_ref, c_ref, glu_ref, gate_ref,
                   h_ref, carry_ref, *, tm):
    i = pl.program_id(1)
    h_ref[...] = _rms(x_ref[...], g_ref[...]).astype(BF16)

    def seg(c0, n):
        return _dot(h_ref[...], w_ref[:, c0:c0 + n])

    au_ref[...] = jax.nn.gelu(seg(0, SGU_WIDTH))
    av_ref[...] = _layer_norm(jax.nn.gelu(seg(SGU_WIDTH, SGU_WIDTH)), lng_ref[...], lnb_ref[...])
    q_ref[...] = seg(R_Q, ATT_WIDTH).astype(BF16)
    kk = seg(R_K, ATT_WIDTH)
    k_ref[...] = kk
    kb_ref[...] = kk.astype(BF16)
    vv = seg(R_V, ATT_WIDTH)
    v_ref[...] = vv
    vb_ref[...] = vv.astype(BF16)
    zg = seg(R_GLU, 2 * CONV_CH)
    glu_ref[...] = zg[:, :CONV_CH] * jax.nn.sigmoid(zg[:, CONV_CH:])
    gate_chunk = 512
    for c in range(N_BRANCH * D_MODEL // gate_chunk):
        gate_ref[:, c * gate_chunk:(c + 1) * gate_chunk] = jax.nn.sigmoid(seg(R_GATE + c * gate_chunk, gate_chunk))
    lf = _log_sigmoid(seg(R_F, LANES) + bf_ref[...])
    lf_ref[...] = lf[:, :N_HEADS]

    @pl.when(i == 0)
    def _():
        carry_ref[...] = jnp.zeros_like(carry_ref)

    c = _dot01(tri_ref[...], lf) + carry_ref[...]
    c_ref[...] = c[:, :N_HEADS]
    carry_ref[...] = c[tm - 1:tm, :]


def _inproj(x2d, g, w, bf, lng, lnb, tri, *, nseq, tm):
    m = x2d.shape[0]
    nt = m // (nseq * tm)
    row = lambda s, i: (s * nt + i, 0)

    def rows(c):
        return pl.BlockSpec((tm, c), row)

    out_shape = (
        jax.ShapeDtypeStruct((m, SGU_WIDTH), F32), jax.ShapeDtypeStruct((m, SGU_WIDTH), F32),
        jax.ShapeDtypeStruct((m, ATT_WIDTH), BF16),
        jax.ShapeDtypeStruct((m, ATT_WIDTH), F32), jax.ShapeDtypeStruct((m, ATT_WIDTH), F32),
        jax.ShapeDtypeStruct((m, ATT_WIDTH), BF16), jax.ShapeDtypeStruct((m, ATT_WIDTH), BF16),
        jax.ShapeDtypeStruct((m, N_HEADS), F32), jax.ShapeDtypeStruct((m, N_HEADS), F32),
        jax.ShapeDtypeStruct((m, CONV_CH), F32), jax.ShapeDtypeStruct((m, N_BRANCH * D_MODEL), F32),
    )
    out_specs = (rows(SGU_WIDTH), rows(SGU_WIDTH), rows(ATT_WIDTH), rows(ATT_WIDTH), rows(ATT_WIDTH),
                 rows(ATT_WIDTH), rows(ATT_WIDTH), rows(N_HEADS), rows(N_HEADS), rows(CONV_CH),
                 rows(N_BRANCH * D_MODEL))
    return pl.pallas_call(
        functools.partial(_inproj_kernel, tm=tm),
        grid=(nseq, nt),
        in_specs=[rows(D_MODEL), _const_spec((1, D_MODEL)), _const_spec((D_MODEL, R_COLS)),
                  _const_spec((1, LANES)), _const_spec((1, SGU_WIDTH)), _const_spec((1, SGU_WIDTH)),
                  _const_spec((tm, tm))],
        out_specs=out_specs,
        out_shape=out_shape,
        scratch_shapes=[pltpu.VMEM((tm, D_MODEL), BF16), pltpu.VMEM((1, LANES), F32)],
        compiler_params=_params(2),
        name="inproj",
    )(x2d, g, w, bf, lng, lnb, tri)


def _attn_prompt_kernel(q_ref, k_ref, v_ref, cq_ref, ck_ref, o_ref, m_ref, l_ref, acc_ref, *, tq):
    i = pl.program_id(2)
    lane = lax.broadcasted_iota(jnp.int32, (1, LANES), 1)
    row_ids = lax.broadcasted_iota(jnp.int32, (tq, tq), 0)
    col_ids = lax.broadcasted_iota(jnp.int32, (tq, tq), 1)
    q = q_ref[...]
    for hh in range(2):
        qm = jnp.where((lane // HEAD_DIM) == hh, q, jnp.zeros_like(q))
        cq = cq_ref[:, hh:hh + 1]
        m_ref[...] = jnp.full_like(m_ref, NEG_INF)
        l_ref[...] = jnp.zeros_like(l_ref)
        acc_ref[...] = jnp.zeros_like(acc_ref)

        def step(j, diagonal):
            start = pl.multiple_of(j * tq, tq)
            ks = k_ref[pl.ds(start, tq), :]
            vs = v_ref[pl.ds(start, tq), :]
            s = _dot_nt(qm, ks) + (cq - ck_ref[hh:hh + 1, pl.ds(start, tq)])
            if diagonal:
                s = jnp.where(col_ids <= row_ids, s, NEG_INF)
            m_old = m_ref[...]
            m_new = jnp.maximum(m_old, jnp.max(s, axis=-1, keepdims=True))
            alpha = jnp.exp(m_old - m_new)
            p = jnp.exp(s - m_new)
            l_ref[...] = alpha * l_ref[...] + jnp.sum(p, axis=-1, keepdims=True)
            acc_ref[...] = alpha * acc_ref[...] + _dot(p.astype(BF16), vs)
            m_ref[...] = m_new

        def body(j, carry):
            step(j, False)
            return carry

        lax.fori_loop(0, i, body, 0)
        step(i, True)
        out = acc_ref[...] / l_ref[...]
        if hh == 0:
            out0 = out
        else:
            o_ref[...] = jnp.where(lane < HEAD_DIM, out0, out).astype(o_ref.dtype)


def _attn_prompt(q, k, v, cq, ck, *, tq):
    b, s, _ = q.shape
    npair = N_HEADS // 2
    return pl.pallas_call(
        functools.partial(_attn_prompt_kernel, tq=tq),
        grid=(b, npair, s // tq),
        in_specs=[
            pl.BlockSpec((None, tq, LANES), lambda bb, p, i: (bb, i, p)),
            pl.BlockSpec((None, s, LANES), lambda bb, p, i: (bb, 0, p)),
            pl.BlockSpec((None, s, LANES), lambda bb, p, i: (bb, 0, p)),
            pl.BlockSpec((None, None, tq, 2), lambda bb, p, i: (bb, p, i, 0)),
            pl.BlockSpec((None, None, 2, s), lambda bb, p, i: (bb, p, 0, 0)),
        ],
        out_specs=pl.BlockSpec((None, tq, LANES), lambda bb, p, i: (bb, i, p)),
        out_shape=jax.ShapeDtypeStruct((b, s, ATT_WIDTH), BF16),
        scratch_shapes=[pltpu.VMEM((tq, 1), F32), pltpu.VMEM((tq, 1), F32), pltpu.VMEM((tq, LANES), F32)],
        compiler_params=_params(3),
        name="attn_prompt",
    )(q, k, v, cq, ck)


def _attn_sample_kernel(pt_ref, q_ref, kn_ref, vn_ref, cn_ref, *rest, npp, t_new):
    del pt_ref
    k_refs = rest[:npp]
    v_refs = rest[npp:2 * npp]
    lf_refs = rest[2 * npp:3 * npp]
    o_ref = rest[3 * npp]
    m_ref, l_ref, acc_ref, run_ref = rest[3 * npp + 1:]
    j = pl.program_id(1)
    nrow = t_new * N_HEADS
    ncol = PAGE_SIZE * N_HEADS
    per_row = LANES // N_HEADS

    @pl.when(j == 0)
    def _():
        m_ref[...] = jnp.full_like(m_ref, NEG_INF)
        l_ref[...] = jnp.zeros_like(l_ref)
        acc_ref[...] = jnp.zeros_like(acc_ref)
        run_ref[...] = jnp.zeros_like(run_ref)

    q = q_ref[...]
    row_head = lax.broadcasted_iota(jnp.int32, (nrow, LANES), 0) % N_HEADS
    lane_head = lax.broadcasted_iota(jnp.int32, (nrow, LANES), 1) % N_HEADS
    same_head = row_head == lane_head
    lane8 = lax.broadcasted_iota(jnp.int32, (SUBLANES, LANES), 1)
    sub8 = lax.broadcasted_iota(jnp.int32, (SUBLANES, LANES), 0)

    def online_update(s, vals):
        m_old = m_ref[...]
        m_new = jnp.maximum(m_old, jnp.max(s, axis=-1, keepdims=True))
        alpha = jnp.exp(m_old - m_new)
        p = jnp.exp(s - m_new)
        l_ref[...] = alpha * l_ref[...] + jnp.sum(p, axis=-1, keepdims=True)
        acc_ref[...] = alpha * acc_ref[...] + _dot(p.astype(BF16), vals)
        m_ref[...] = m_new

    for pi in range(npp):
        lf = lf_refs[pi][...]
        suf = lf
        pre = lf
        k = N_HEADS
        while k < LANES:
            suf = suf + jnp.where(lane8 + k < LANES, pltpu.roll(suf, LANES - k, 1), 0.0)
            pre = pre + jnp.where(lane8 - k >= 0, pltpu.roll(pre, k, 1), 0.0)
            k *= 2
        row_tot = suf + pre - lf
        later = row_tot
        k = 1
        while k < SUBLANES:
            later = later + jnp.where(sub8 + k < SUBLANES, pltpu.roll(later, SUBLANES - k, 0), 0.0)
            k *= 2
        run = run_ref[...]
        r = (suf - lf) + (later - row_tot) + run
        run_ref[...] = run + later[0:1, :]

        kp = k_refs[pi][...].reshape(ncol, HEAD_DIM).astype(BF16)
        vp = v_refs[pi][...].reshape(ncol, HEAD_DIM).astype(BF16)
        logits = _dot_nt(q, kp)
        pieces = []
        for cb in range(ncol // LANES):
            blk = logits[:, cb * LANES:(cb + 1) * LANES] + r[cb:cb + 1, :]
            pieces.append(jnp.where(same_head, blk, NEG_INF))
        online_update(jnp.concatenate(pieces, axis=-1), vp)

    @pl.when(j == pl.num_programs(1) - 1)
    def _():
        kn = kn_ref[...]
        ln = _dot_nt(q, kn) - cn_ref[...]
        rr = lax.broadcasted_iota(jnp.int32, (nrow, nrow), 0)
        cc = lax.broadcasted_iota(jnp.int32, (nrow, nrow), 1)
        ok = (rr % N_HEADS == cc % N_HEADS) & (cc // N_HEADS <= rr // N_HEADS)
        online_update(jnp.where(ok, ln, NEG_INF), vn_ref[...])
        o_ref[...] = acc_ref[...] / l_ref[...]


def _attn_sample(page_table, layer, q, kn, vn, cn, cache_k, cache_v, cache_lf, *, npp):
    bd, nrow, _ = q.shape
    n_pages = page_table.shape[1]
    nsteps = n_pages // npp
    t_new = nrow // N_HEADS

    def page_idx(b, j, pt, pi):
        return pt[b, n_pages - 1 - (j * npp + pi)]

    def kv_spec(pi):
        return pl.BlockSpec((None, None, PAGE_SIZE, N_HEADS, HEAD_DIM),
                            lambda b, j, pt: (page_idx(b, j, pt, pi), layer, 0, 0, 0))

    def lf_spec(pi):
        return pl.BlockSpec((None, None, SUBLANES, LANES),
                            lambda b, j, pt: (page_idx(b, j, pt, pi), layer, 0, 0))

    seq_spec = lambda c: pl.BlockSpec((None, nrow, c), lambda b, j, pt: (b, 0, 0))
    grid_spec = pltpu.PrefetchScalarGridSpec(
        num_scalar_prefetch=1,
        grid=(bd, nsteps),
        in_specs=[seq_spec(HEAD_DIM), seq_spec(HEAD_DIM), seq_spec(HEAD_DIM),
                  pl.BlockSpec((None, 1, nrow), lambda b, j, pt: (b, 0, 0))]
        + [kv_spec(pi) for pi in range(npp)] + [kv_spec(pi) for pi in range(npp)]
        + [lf_spec(pi) for pi in range(npp)],
        out_specs=seq_spec(HEAD_DIM),
        scratch_shapes=[pltpu.VMEM((nrow, 1), F32), pltpu.VMEM((nrow, 1), F32),
                        pltpu.VMEM((nrow, HEAD_DIM), F32), pltpu.VMEM((1, LANES), F32)],
    )
    return pl.pallas_call(
        functools.partial(_attn_sample_kernel, npp=npp, t_new=t_new),
        grid_spec=grid_spec,
        out_shape=jax.ShapeDtypeStruct((bd, nrow, HEAD_DIM), F32),
        compiler_params=_params(2),
        name="attn_sample",
    )(page_table, q, kn, vn, cn, *([cache_k] * npp), *([cache_v] * npp), *([cache_lf] * npp))


def _mix_kernel(au_ref, av_ref, at_ref, glu_ref, gate_ref, x_ref, hist_ref,
                mixw_ref, mixb_ref, wa_ref, wb_ref, cw_ref, cb_ref, cg_ref, cbeta_ref, wc_ref, wo_ref,
                xo_ref, cstate_ref, buf_ref, ua_ref, cs_ref, *, tm, rs, hr):
    i = pl.program_id(1)

    @pl.when(i == 0)
    def _():
        buf_ref[0:hr, :] = hist_ref[...]

    buf_ref[hr:hr + tm, :] = glu_ref[...]

    off = hr - (CONV_W - 1) * rs
    rc = 64
    for r0 in range(0, tm, rc):
        acc = jnp.broadcast_to(cb_ref[...], (rc, CONV_CH))
        for j in range(CONV_W):
            acc = acc + cw_ref[j:j + 1, :] * buf_ref[off + j * rs + r0:off + j * rs + r0 + rc, :]
        y = _layer_norm(acc, cg_ref[...], cbeta_ref[...])
        cs_ref[r0:r0 + rc, :] = (y * jax.nn.sigmoid(y)).astype(BF16)
    cstate_ref[...] = buf_ref[tm:tm + hr, :]
    buf_ref[0:hr, :] = buf_ref[tm:tm + hr, :]

    lane_group = lax.broadcasted_iota(jnp.int32, (CHUNK, SGU_WIDTH), 1) // (SGU_WIDTH // SGU_GROUPS)
    for c0 in range(0, tm, CHUNK):
        vc = av_ref[c0:c0 + CHUNK, :].astype(BF16)
        mixed = _dot(mixw_ref[0], vc)
        for g in range(1, SGU_GROUPS):
            mixed = jnp.where(lane_group == g, _dot(mixw_ref[g], vc), mixed)
        ua_ref[c0:c0 + CHUNK, :] = (au_ref[c0:c0 + CHUNK, :] * (mixed + mixb_ref[...])).astype(BF16)

    ya = _dot(ua_ref[...], wa_ref[...])
    yb = _dot(at_ref[...], wb_ref[...])
    yc = _dot(cs_ref[...], wc_ref[...])
    merged = (gate_ref[:, 0:D_MODEL] * ya + gate_ref[:, D_MODEL:2 * D_MODEL] * yb
              + gate_ref[:, 2 * D_MODEL:3 * D_MODEL] * yc)
    xo_ref[...] = x_ref[...] + _dot(merged.astype(BF16), wo_ref[...])


def _mix(au, av, attn, glu, gate, x2d, hist, mixw, mixb, wa, wb, cw, cb, cg, cbeta, wc, wo, *, nseq, tm, rs):
    m = x2d.shape[0]
    nt = m // (nseq * tm)
    hr = hist.shape[1]
    row = lambda s, i: (s * nt + i, 0)
    rows = lambda c: pl.BlockSpec((tm, c), row)
    return pl.pallas_call(
        functools.partial(_mix_kernel, tm=tm, rs=rs, hr=hr),
        grid=(nseq, nt),
        in_specs=[rows(SGU_WIDTH), rows(SGU_WIDTH), rows(ATT_WIDTH), rows(CONV_CH), rows(N_BRANCH * D_MODEL),
                  rows(D_MODEL), pl.BlockSpec((None, hr, CONV_CH), lambda s, i: (s, 0, 0)),
                  _const_spec((SGU_GROUPS, CHUNK, CHUNK)), _const_spec((CHUNK, SGU_WIDTH)),
                  _const_spec((SGU_WIDTH, D_MODEL)), _const_spec((ATT_WIDTH, D_MODEL)),
                  _const_spec((CONV_W + 1, CONV_CH)), _const_spec((1, CONV_CH)), _const_spec((1, CONV_CH)),
                  _const_spec((1, CONV_CH)), _const_spec((CONV_CH, D_MODEL)), _const_spec((D_MODEL, D_MODEL))],
        out_specs=(rows(D_MODEL), pl.BlockSpec((None, hr, CONV_CH), lambda s, i: (s, 0, 0))),
        out_shape=(jax.ShapeDtypeStruct((m, D_MODEL), F32), jax.ShapeDtypeStruct((nseq, hr, CONV_CH), F32)),
        scratch_shapes=[pltpu.VMEM((hr + tm, CONV_CH), F32), pltpu.VMEM((tm, SGU_WIDTH), BF16),
                        pltpu.VMEM((tm, CONV_CH), BF16)],
        compiler_params=_params(2),
        name="mix",
    )(au, av, attn, glu, gate, x2d, hist, mixw, mixb, wa, wb, cw, cb, cg, cbeta, wc, wo)


def _ffn_kernel(x_ref, g_ref, hist_ref, wup_ref, fcw_ref, fcb_ref, wdn_ref, gf_ref,
                xo_ref, fstate_ref, h_ref, carry_ref, buf_ref, acc_ref, *, tm, rs, hr, ck, final_norm):
    i = pl.program_id(1)

    @pl.when(i == 0)
    def _():
        carry_ref[...] = hist_ref[...]

    x = x_ref[...]
    h_ref[...] = _rms(x, g_ref[...]).astype(BF16)
    acc_ref[...] = x
    for c in range(D_FF // ck):
        halves = []
        for half in range(2):
            col = half * D_FF + c * ck
            buf_ref[half, 0:hr, :] = carry_ref[:, col:col + ck]
            buf_ref[half, hr:hr + tm, :] = _dot(h_ref[...], wup_ref[:, col:col + ck])
            carry_ref[:, col:col + ck] = buf_ref[half, tm:tm + hr, :]
            conv = fcb_ref[:, col:col + ck]
            for j in range(FFN_CONV_W):
                o = hr - (FFN_CONV_W - 1 - j) * rs
                conv = conv + fcw_ref[j:j + 1, col:col + ck] * buf_ref[half, o:o + tm, :]
            halves.append(conv)
        act = (jax.nn.gelu(halves[0]) * halves[1]).astype(BF16)
        acc_ref[...] += _dot(act, wdn_ref[c * ck:(c + 1) * ck, :])
    fstate_ref[...] = carry_ref[...]
    out = acc_ref[...]
    if final_norm:
        out = _rms(out, gf_ref[...])
    xo_ref[...] = out


def _ffn(x2d, g, hist, wup, fcw, fcb, wdn, gf, *, nseq, tm, rs, final_norm):
    m = x2d.shape[0]
    nt = m // (nseq * tm)
    hr = hist.shape[1]
    ck = 256
    row = lambda s, i: (s * nt + i, 0)
    rows = lambda c: pl.BlockSpec((tm, c), row)
    state_spec = pl.BlockSpec((None, hr, 2 * D_FF), lambda s, i: (s, 0, 0))
    return pl.pallas_call(
        functools.partial(_ffn_kernel, tm=tm, rs=rs, hr=hr, ck=ck, final_norm=final_norm),
        grid=(nseq, nt),
        in_specs=[rows(D_MODEL), _const_spec((1, D_MODEL)), state_spec,
                  _const_spec((D_MODEL, 2 * D_FF)), _const_spec((SUBLANES, 2 * D_FF)),
                  _const_spec((1, 2 * D_FF)), _const_spec((D_FF, D_MODEL)), _const_spec((1, D_MODEL))],
        out_specs=(rows(D_MODEL), state_spec),
        out_shape=(jax.ShapeDtypeStruct((m, D_MODEL), F32), jax.ShapeDtypeStruct((nseq, hr, 2 * D_FF), F32)),
        scratch_shapes=[pltpu.VMEM((tm, D_MODEL), BF16), pltpu.VMEM((hr, 2 * D_FF), F32),
                        pltpu.VMEM((2, hr + tm, ck), F32), pltpu.VMEM((tm, D_MODEL), F32)],
        compiler_params=_params(2),
        name="ffn",
    )(x2d, g, hist, wup, fcw, fcb, wdn, gf)


def _round_up(n, k):
    return -(-n // k) * k


def _layer_weights(l, norm1_g, w_in, b_forget, sgu_ln_g, sgu_ln_b, sgu_w, sgu_b, w_proj_a, w_proj_b,
                   conv_w, conv_b, conv_ln_g, conv_ln_b, w_proj_c, w_out, norm2_g, w_up,
                   ffn_conv_w, ffn_conv_b, w_down):
    w = w_in[l]
    scale = HEAD_DIM ** -0.5
    w_r = jnp.concatenate([
        w[:, :R_Q], w[:, R_Q:R_K] * scale, w[:, R_K:OFF_F], w[:, OFF_GLU:IN_COLS],
        w[:, OFF_F:OFF_GLU], jnp.zeros((D_MODEL, LANES - N_HEADS), F32)], axis=1).astype(BF16)
    bf = jnp.pad(b_forget[l], (0, LANES - N_HEADS))[None, :]
    causal = jnp.tril(jnp.ones((CHUNK, CHUNK), dtype=bool))
    return dict(
        g1=norm1_g[l][None, :], w_in=w_r, bf=bf, lng=sgu_ln_g[l][None, :], lnb=sgu_ln_b[l][None, :],
        sgu_w=jnp.where(causal, sgu_w[l], 0.0), sgu_b=sgu_b[l],
        wa=w_proj_a[l].astype(BF16), wb=w_proj_b[l].astype(BF16), wc=w_proj_c[l].astype(BF16),
        cw=jnp.pad(conv_w[l], ((0, 1), (0, 0))), cb=conv_b[l][None, :],
        cg=conv_ln_g[l][None, :], cbeta=conv_ln_b[l][None, :],
        wo=w_out[l].astype(BF16), g2=norm2_g[l][None, :], wup=w_up[l].astype(BF16),
        fcw=jnp.pad(ffn_conv_w[l], ((0, SUBLANES - FFN_CONV_W), (0, 0))), fcb=ffn_conv_b[l][None, :],
        wdn=w_down[l].astype(BF16))


def _mix_bias(sgu_b_rows):
    return jnp.repeat(sgu_b_rows.T, SGU_WIDTH // SGU_GROUPS, axis=1)


def kernel(x_prompt, x_sample, cache_k, cache_v, cache_logf, state_conv, state_ffn, page_table, norm1_g, w_in, b_forget, sgu_ln_g, sgu_ln_b, sgu_w, sgu_b, w_proj_a, w_proj_b, conv_w, conv_b, conv_ln_g, conv_ln_b, w_proj_c, w_out, norm2_g, w_up, ffn_conv_w, ffn_conv_b, w_down, norm_f_g):
    bp, seq, _ = x_prompt.shape
    bd, t_new, _ = x_sample.shape
    depth = w_in.shape[0]
    n_phys = cache_k.shape[0]
    assert seq % CHUNK == 0 and PAGE_SIZE == CHUNK and t_new <= CHUNK
    assert seq >= CONV_W - 1 and t_new >= FFN_CONV_W - 1 and t_new <= CONV_W - 1

    tm_p = 256
    tq = 512
    ms = bd * t_new
    assert ms == CHUNK and page_table.shape[1] % PAGES_PER_STEP == 0

    hr_conv_p = _round_up(CONV_W - 1, SUBLANES)
    hr_ffn_p = _round_up(FFN_CONV_W - 1, SUBLANES)
    hr_conv_s = (CONV_W - 1) * bd
    hr_ffn_s = (FFN_CONV_W - 1) * bd

    tri_p = jnp.tril(jnp.ones((tm_p, tm_p), F32)).astype(BF16)
    eye_b = jnp.eye(bd, dtype=F32)
    tri_s = jnp.kron(jnp.tril(jnp.ones((t_new, t_new), F32)), eye_b).astype(BF16)
    cache_lf = cache_logf.reshape(n_phys, depth, SUBLANES, PAGE_SIZE * N_HEADS // SUBLANES)
    gf = norm_f_g[None, :]

    xp = x_prompt.reshape(bp * seq, D_MODEL)
    xs = jnp.transpose(x_sample, (1, 0, 2)).reshape(ms, D_MODEL)

    def to_seq_major(a):
        return jnp.transpose(a.reshape(t_new, bd, a.shape[-1]), (1, 0, 2))

    outs = {k: [] for k in ("kp", "vp", "fp", "cp", "ffp", "ks", "vs", "fs", "avs", "cs", "ffs")}
    for l in range(depth):
        lw = _layer_weights(l, norm1_g, w_in, b_forget, sgu_ln_g, sgu_ln_b, sgu_w, sgu_b, w_proj_a, w_proj_b,
                            conv_w, conv_b, conv_ln_g, conv_ln_b, w_proj_c, w_out, norm2_g, w_up,
                            ffn_conv_w, ffn_conv_b, w_down)
        last = l == depth - 1

        au, av, q, k, v, kb, vb, lf, c, glu, gate = _inproj(
            xp, lw["g1"], lw["w_in"], lw["bf"], lw["lng"], lw["lnb"], tri_p, nseq=bp, tm=tm_p)
        c4 = c.reshape(bp, seq, N_HEADS // 2, 2)
        attn = _attn_prompt(q.reshape(bp, seq, ATT_WIDTH), kb.reshape(bp, seq, ATT_WIDTH),
                            vb.reshape(bp, seq, ATT_WIDTH), jnp.transpose(c4, (0, 2, 1, 3)),
                            jnp.transpose(c4, (0, 2, 3, 1)), tq=tq)
        xp, cstate = _mix(au, av, attn.reshape(bp * seq, ATT_WIDTH), glu, gate, xp,
                          jnp.zeros((bp, hr_conv_p, CONV_CH), F32),
                          lw["sgu_w"].astype(BF16), _mix_bias(lw["sgu_b"]),
                          lw["wa"], lw["wb"], lw["cw"], lw["cb"], lw["cg"], lw["cbeta"], lw["wc"], lw["wo"],
                          nseq=bp, tm=tm_p, rs=1)
        xp, fstate = _ffn(xp, lw["g2"], jnp.zeros((bp, hr_ffn_p, 2 * D_FF), F32), lw["wup"], lw["fcw"],
                          lw["fcb"], lw["wdn"], gf, nseq=bp, tm=tm_p, rs=1, final_norm=last)
        outs["kp"].append(k.reshape(bp, seq, N_HEADS, HEAD_DIM))
        outs["vp"].append(v.reshape(bp, seq, N_HEADS, HEAD_DIM))
        outs["fp"].append(lf.reshape(bp, seq, N_HEADS))
        outs["cp"].append(cstate[:, hr_conv_p - (CONV_W - 1):])
        outs["ffp"].append(fstate[:, hr_ffn_p - (FFN_CONV_W - 1):])

        au, av, q, k, v, kb, vb, lf, c, glu, gate = _inproj(
            xs, lw["g1"], lw["w_in"], lw["bf"], lw["lng"], lw["lnb"], tri_s, nseq=1, tm=ms)
        rows_th = lambda a: to_seq_major(a).reshape(bd, t_new * N_HEADS, HEAD_DIM)
        cn = to_seq_major(c).reshape(bd, 1, t_new * N_HEADS)
        attn = _attn_sample(page_table, l, rows_th(q), rows_th(kb), rows_th(vb), cn,
                            cache_k, cache_v, cache_lf, npp=PAGES_PER_STEP)
        attn = jnp.transpose(attn.reshape(bd, t_new, ATT_WIDTH), (1, 0, 2)).reshape(ms, ATT_WIDTH).astype(BF16)
        mixw_s = jnp.stack([jnp.kron(lw["sgu_w"][g, :t_new, :t_new], eye_b) for g in range(SGU_GROUPS)])
        mixb_s = _mix_bias(jnp.repeat(lw["sgu_b"][:, :t_new], bd, axis=1))
        hist_c = jnp.transpose(state_conv[:, l], (1, 0, 2)).reshape(1, hr_conv_s, CONV_CH)
        xs, cstate = _mix(au, av, attn, glu, gate, xs, hist_c, mixw_s.astype(BF16), mixb_s,
                          lw["wa"], lw["wb"], lw["cw"], lw["cb"], lw["cg"], lw["cbeta"], lw["wc"], lw["wo"],
                          nseq=1, tm=ms, rs=bd)
        hist_f = jnp.transpose(state_ffn[:, l], (1, 0, 2)).reshape(1, hr_ffn_s, 2 * D_FF)
        xs, fstate = _ffn(xs, lw["g2"], hist_f, lw["wup"], lw["fcw"], lw["fcb"], lw["wdn"], gf,
                          nseq=1, tm=ms, rs=bd, final_norm=last)
        outs["ks"].append(to_seq_major(k).reshape(bd, t_new, N_HEADS, HEAD_DIM))
        outs["vs"].append(to_seq_major(v).reshape(bd, t_new, N_HEADS, HEAD_DIM))
        outs["fs"].append(to_seq_major(lf))
        outs["avs"].append(to_seq_major(av))
        outs["cs"].append(jnp.transpose(cstate.reshape(CONV_W - 1, bd, CONV_CH), (1, 0, 2)))
        outs["ffs"].append(jnp.transpose(fstate.reshape(FFN_CONV_W - 1, bd, 2 * D_FF), (1, 0, 2)))

    y_prompt = xp.reshape(bp, seq, D_MODEL)
    y_sample = to_seq_major(xs)
    st = lambda key: jnp.stack(outs[key], axis=1)
    return (y_prompt, y_sample, st("kp"), st("vp"), st("fp"), st("cp"), st("ffp"),
            st("ks"), st("vs"), st("fs"), st("avs"), st("cs"), st("ffs"))
```

```python
import functools

import jax
import jax.numpy as jnp
from jax import lax
from jax.experimental import pallas as pl
from jax.experimental.pallas import tpu as pltpu

F32 = jnp.float32
BF16 = jnp.bfloat16

D_MODEL = 1024
SGU_WIDTH = 256
SGU_GROUPS = 4
CHUNK = 128
N_HEADS = 8
HEAD_DIM = 64
ATT_WIDTH = N_HEADS * HEAD_DIM
CONV_CH = 256
CONV_W = 31
D_FF = 2816
FFN_CONV_W = 3
N_BRANCH = 3
PAGE_SIZE = 128
EPS = 1e-6
NEG_INF = -1e30

OFF_F = 2 * SGU_WIDTH + 3 * ATT_WIDTH
OFF_GLU = OFF_F + N_HEADS
IN_COLS = OFF_GLU + 2 * CONV_CH + N_BRANCH * D_MODEL

LANES = 128
SUBLANES = 8
R_Q = 2 * SGU_WIDTH
R_K = R_Q + ATT_WIDTH
R_V = R_K + ATT_WIDTH
R_GLU = R_V + ATT_WIDTH
R_GATE = R_GLU + 2 * CONV_CH
R_F = R_GATE + N_BRANCH * D_MODEL
R_COLS = R_F + LANES

PAIR_X = 2 * LANES
QKX_WIDTH = (N_HEADS // 2) * PAIR_X
AUG_LANES = 6 * N_HEADS

VMEM_LIMIT_BYTES = 56 * 1024 * 1024
PAGES_PER_STEP = 8


def _dot(a, b):
    return jnp.dot(a, b, preferred_element_type=F32)


def _dot_nt(a, b):
    return lax.dot_general(a, b, (((1,), (1,)), ((), ())), preferred_element_type=F32)


def _split3(x):
    hi = x.astype(BF16)
    r1 = x - hi.astype(F32)
    mid = r1.astype(BF16)
    lo = (r1 - mid.astype(F32)).astype(BF16)
    return hi, mid, lo


def _dot01(m01, x):
    hi, mid, lo = _split3(x)
    return _dot(m01, hi) + _dot(m01, mid) + _dot(m01, lo)


def _rms(x, g):
    return x * lax.rsqrt(jnp.mean(x * x, axis=-1, keepdims=True) + EPS) * g


def _layer_norm(x, g, b):
    mu = jnp.mean(x, axis=-1, keepdims=True)
    xc = x - mu
    var = jnp.mean(xc * xc, axis=-1, keepdims=True)
    return xc * lax.rsqrt(var + EPS) * g + b


def _log_sigmoid(x):
    return jnp.minimum(x, 0.0) - jnp.log1p(jnp.exp(-jnp.abs(x)))


def _const_spec(shape):
    nd = len(shape)
    return pl.BlockSpec(shape, lambda *_: (0,) * nd, pipeline_mode=pl.Buffered(1))


def _params(n_axes):
    return pltpu.CompilerParams(dimension_semantics=("arbitrary",) * n_axes,
                                vmem_limit_bytes=VMEM_LIMIT_BYTES)


def _inproj_kernel(x_ref, g_ref, w_ref, bf_ref, lng_ref, lnb_ref, tri_ref, kt_in_ref, vt_in_ref,
                   au_ref, av_ref, qx_ref, kx_ref, lf_ref, c_ref, glu_ref, gate_ref, kt_ref, vt_ref,
                   h_ref, carry_ref, *, tm):
    del kt_in_ref, vt_in_ref
    i = pl.program_id(1)
    h_ref[...] = _rms(x_ref[...], g_ref[...]).astype(BF16)

    def seg(c0, n):
        return _dot(h_ref[...], w_ref[:, c0:c0 + n])

    au_ref[...] = jax.nn.gelu(seg(0, SGU_WIDTH))
    av_ref[...] = _layer_norm(jax.nn.gelu(seg(SGU_WIDTH, SGU_WIDTH)), lng_ref[...], lnb_ref[...])
    qq = seg(R_Q, ATT_WIDTH).astype(BF16)
    kk = seg(R_K, ATT_WIDTH)
    kt_ref[...] = kk.T
    kkb = kk.astype(BF16)
    for p in range(N_HEADS // 2):
        qx_ref[:, p * PAIR_X:p * PAIR_X + LANES] = qq[:, p * LANES:(p + 1) * LANES]
        kx_ref[:, p * PAIR_X:p * PAIR_X + LANES] = kkb[:, p * LANES:(p + 1) * LANES]
    vt_ref[...] = seg(R_V, ATT_WIDTH).T
    zg = seg(R_GLU, 2 * CONV_CH)
    glu_ref[...] = zg[:, :CONV_CH] * jax.nn.sigmoid(zg[:, CONV_CH:])
    gate_chunk = 512
    for c in range(N_BRANCH * D_MODEL // gate_chunk):
        gate_ref[:, c * gate_chunk:(c + 1) * gate_chunk] = jax.nn.sigmoid(seg(R_GATE + c * gate_chunk, gate_chunk))
    lf = _log_sigmoid(seg(R_F, LANES) + bf_ref[...])
    lf_ref[...] = lf[:, :N_HEADS]

    @pl.when(i == 0)
    def _():
        carry_ref[...] = jnp.zeros_like(carry_ref)

    c = _dot01(tri_ref[...], lf) + carry_ref[...]
    c_ref[...] = c[:, :N_HEADS]
    carry_ref[...] = c[tm - 1:tm, :]

    lane = lax.broadcasted_iota(jnp.int32, (1, LANES), 1)
    hi, mid, lo = (part.astype(F32) for part in _split3(jnp.where(lane < N_HEADS, c, 0.0)))
    kaug = (jnp.where((lane >= 3 * N_HEADS) & (lane < AUG_LANES), 1.0, 0.0)
            - (hi + pltpu.roll(mid, N_HEADS, 1) + pltpu.roll(lo, 2 * N_HEADS, 1))).astype(BF16)
    qaug = (jnp.where(lane < 3 * N_HEADS, 1.0, 0.0) + pltpu.roll(hi, 3 * N_HEADS, 1)
            + pltpu.roll(mid, 4 * N_HEADS, 1) + pltpu.roll(lo, 5 * N_HEADS, 1)).astype(BF16)
    for p in range(N_HEADS // 2):
        qx_ref[:, p * PAIR_X + LANES:(p + 1) * PAIR_X] = qaug
        kx_ref[:, p * PAIR_X + LANES:(p + 1) * PAIR_X] = kaug


def _inproj(x2d, g, w, bf, lng, lnb, tri, kt_all, vt_all, layer, *, nseq, tm):
    m = x2d.shape[0]
    nt = m // (nseq * tm)
    row = lambda s, i: (s * nt + i, 0)

    def rows(c):
        return pl.BlockSpec((tm, c), row)

    out_shape = (
        jax.ShapeDtypeStruct((m, SGU_WIDTH), F32), jax.ShapeDtypeStruct((m, SGU_WIDTH), F32),
        jax.ShapeDtypeStruct((m, QKX_WIDTH), BF16), jax.ShapeDtypeStruct((m, QKX_WIDTH), BF16),
        jax.ShapeDtypeStruct((m, N_HEADS), F32), jax.ShapeDtypeStruct((m, N_HEADS), F32),
        jax.ShapeDtypeStruct((m, CONV_CH), F32), jax.ShapeDtypeStruct((m, N_BRANCH * D_MODEL), F32),
        jax.ShapeDtypeStruct(kt_all.shape, F32), jax.ShapeDtypeStruct(vt_all.shape, F32),
    )
    slab = pl.BlockSpec((None, None, ATT_WIDTH, tm), lambda s, i: (s, layer, 0, i))
    out_specs = (rows(SGU_WIDTH), rows(SGU_WIDTH), rows(QKX_WIDTH), rows(QKX_WIDTH), rows(N_HEADS),
                 rows(N_HEADS), rows(CONV_CH), rows(N_BRANCH * D_MODEL), slab, slab)
    any_spec = pl.BlockSpec(memory_space=pl.ANY)
    return pl.pallas_call(
        functools.partial(_inproj_kernel, tm=tm),
        grid=(nseq, nt),
        in_specs=[rows(D_MODEL), _const_spec((1, D_MODEL)), _const_spec((D_MODEL, R_COLS)),
                  _const_spec((1, LANES)), _const_spec((1, SGU_WIDTH)), _const_spec((1, SGU_WIDTH)),
                  _const_spec((tm, tm)), any_spec, any_spec],
        out_specs=out_specs,
        out_shape=out_shape,
        input_output_aliases={7: 8, 8: 9},
        scratch_shapes=[pltpu.VMEM((tm, D_MODEL), BF16), pltpu.VMEM((1, LANES), F32)],
        compiler_params=_params(2),
        name="inproj",
    )(x2d, g, w, bf, lng, lnb, tri, kt_all, vt_all)


def _attn_prompt_kernel(qx_ref, kx_ref, vt_ref, o_ref, qm_ref, m_ref, l_ref, acc_ref, *, tq, pairs):
    grp = pl.program_id(1)
    i = pl.program_id(2)
    nh = 2 * pairs
    lane_x = lax.broadcasted_iota(jnp.int32, (1, PAIR_X), 1)
    key_ids = lax.broadcasted_iota(jnp.int32, (tq, tq), 0)
    qry_ids = lax.broadcasted_iota(jnp.int32, (tq, tq), 1)
    for hl in range(nh):
        pr, hh = divmod(hl, 2)
        head = grp * nh + hl
        own = ((lane_x // HEAD_DIM) == hh) | (
            (lane_x >= LANES) & (lane_x < LANES + AUG_LANES) & ((lane_x - LANES) % N_HEADS == head))
        qm_ref[hl] = qx_ref[:, pr * PAIR_X:(pr + 1) * PAIR_X] * jnp.where(own, 1.0, 0.0).astype(BF16)
    m_ref[...] = jnp.full_like(m_ref, NEG_INF)
    l_ref[...] = jnp.zeros_like(l_ref)
    acc_ref[...] = jnp.zeros_like(acc_ref)

    def step(j, diagonal):
        start = pl.multiple_of(j * tq, tq)
        for hl in range(nh):
            pr = hl // 2
            ks = kx_ref[pl.ds(start, tq), pr * PAIR_X:(pr + 1) * PAIR_X]
            st = _dot_nt(ks, qm_ref[hl])
            if diagonal:
                st = jnp.where(key_ids <= qry_ids, st, NEG_INF)
            m_old = m_ref[hl]
            m_new = jnp.maximum(m_old, jnp.max(st, axis=0, keepdims=True))
            alpha = jnp.exp(m_old - m_new)
            p = jnp.exp(st - m_new)
            l_ref[hl] = alpha * l_ref[hl] + jnp.sum(p, axis=0, keepdims=True)
            vt = vt_ref[hl * HEAD_DIM:(hl + 1) * HEAD_DIM, pl.ds(start, tq)].astype(BF16)
            acc_ref[hl] = alpha * acc_ref[hl] + _dot(vt, p.astype(BF16))
            m_ref[hl] = m_new

    def body(j, carry):
        step(j, False)
        return carry

    lax.fori_loop(0, i, body, 0)
    step(i, True)
    for pr in range(pairs):
        out_t = jnp.concatenate([acc_ref[2 * pr] / l_ref[2 * pr], acc_ref[2 * pr + 1] / l_ref[2 * pr + 1]], axis=0)
        o_ref[:, pr * LANES:(pr + 1) * LANES] = out_t.T.astype(o_ref.dtype)


def _attn_prompt(qx, kx, vt_all, layer, *, tq, pairs):
    b, s, _ = qx.shape
    ngrp = N_HEADS // (2 * pairs)
    nh = 2 * pairs
    return pl.pallas_call(
        functools.partial(_attn_prompt_kernel, tq=tq, pairs=pairs),
        grid=(b, ngrp, s // tq),
        in_specs=[
            pl.BlockSpec((None, tq, pairs * PAIR_X), lambda bb, g, i: (bb, i, g)),
            pl.BlockSpec((None, s, pairs * PAIR_X), lambda bb, g, i: (bb, 0, g)),
            pl.BlockSpec((None, None, pairs * LANES, s), lambda bb, g, i: (bb, layer, g, 0)),
        ],
        out_specs=pl.BlockSpec((None, tq, pairs * LANES), lambda bb, g, i: (bb, i, g)),
        out_shape=jax.ShapeDtypeStruct((b, s, ATT_WIDTH), BF16),
        scratch_shapes=[pltpu.VMEM((nh, tq, PAIR_X), BF16), pltpu.VMEM((nh, 1, tq), F32),
                        pltpu.VMEM((nh, 1, tq), F32), pltpu.VMEM((nh, HEAD_DIM, tq), F32)],
        compiler_params=_params(3),
        name="attn_prompt",
    )(qx, kx, vt_all)


def _attn_sample_kernel(pt_ref, q_ref, kn_ref, vn_ref, cn_ref, *rest, npp, t_new):
    del pt_ref
    k_refs = rest[:npp]
    v_refs = rest[npp:2 * npp]
    lf_refs = rest[2 * npp:3 * npp]
    o_ref = rest[3 * npp]
    m_ref, l_ref, acc_ref, run_ref = rest[3 * npp + 1:]
    j = pl.program_id(1)
    nrow = t_new * N_HEADS

    @pl.when(j == 0)
    def _():
        m_ref[...] = jnp.full_like(m_ref, NEG_INF)
        l_ref[...] = jnp.zeros_like(l_ref)
        acc_ref[...] = jnp.zeros_like(acc_ref)
        run_ref[...] = jnp.zeros_like(run_ref)

    q = q_ref[...]
    lane8 = lax.broadcasted_iota(jnp.int32, (N_HEADS, PAGE_SIZE), 1)

    def online_update(s, pv):
        m_old = m_ref[...]
        m_new = jnp.maximum(m_old, jnp.max(s, axis=-1, keepdims=True))
        alpha = jnp.exp(m_old - m_new)
        p = jnp.exp(s - m_new)
        l_ref[...] = alpha * l_ref[...] + jnp.sum(p, axis=-1, keepdims=True)
        acc_ref[...] = alpha * acc_ref[...] + pv(p.astype(BF16))
        m_ref[...] = m_new

    run = run_ref[...]
    pieces = []
    for pi in range(npp):
        lf = lf_refs[pi][...]
        suf = lf
        k = 1
        while k < PAGE_SIZE:
            suf = suf + jnp.where(lane8 + k < PAGE_SIZE, pltpu.roll(suf, PAGE_SIZE - k, 1), 0.0)
            k *= 2
        r = (suf - lf) + run
        run = run + jnp.sum(lf, axis=-1, keepdims=True)
        kp = k_refs[pi][...].reshape(ATT_WIDTH, PAGE_SIZE).astype(BF16)
        pieces.append(_dot(q, kp) + jnp.concatenate([r] * t_new, axis=0))
    run_ref[...] = run

    def pv_pages(p):
        out = None
        for pi in range(npp):
            vp = v_refs[pi][...].reshape(ATT_WIDTH, PAGE_SIZE).astype(BF16)
            term = _dot_nt(p[:, pi * PAGE_SIZE:(pi + 1) * PAGE_SIZE], vp)
            out = term if out is None else out + term
        return out

    online_update(jnp.concatenate(pieces, axis=-1), pv_pages)

    @pl.when(j == pl.num_programs(1) - 1)
    def _():
        ln = _dot_nt(q, kn_ref[...]) - cn_ref[...]
        rr = lax.broadcasted_iota(jnp.int32, ln.shape, 0)
        cc = lax.broadcasted_iota(jnp.int32, ln.shape, 1)
        online_update(jnp.where(cc <= rr // N_HEADS, ln, NEG_INF), lambda p: _dot(p, vn_ref[...]))
        full = acc_ref[...] / l_ref[...]
        row_head = lax.broadcasted_iota(jnp.int32, full.shape, 0) % N_HEADS
        lane_head = lax.broadcasted_iota(jnp.int32, full.shape, 1) // HEAD_DIM
        full = jnp.where(row_head == lane_head, full, 0.0)
        for t in range(t_new):
            o_ref[t:t + 1, :] = jnp.sum(full[t * N_HEADS:(t + 1) * N_HEADS], axis=0, keepdims=True)


def _attn_sample(page_table, layer, q, kn, vn, cn, cache_kt, cache_vt, cache_lft, *, npp):
    bd, nrow, _ = q.shape
    n_pages = page_table.shape[1]
    nsteps = n_pages // npp
    t_new = nrow // N_HEADS
    n_new = kn.shape[1]

    def page_idx(b, j, pt, pi):
        return pt[b, n_pages - 1 - (j * npp + pi)]

    def kv_spec(pi):
        return pl.BlockSpec((None, None, N_HEADS, HEAD_DIM, PAGE_SIZE),
                            lambda b, j, pt: (page_idx(b, j, pt, pi), layer, 0, 0, 0))

    def lf_spec(pi):
        return pl.BlockSpec((None, None, N_HEADS, PAGE_SIZE),
                            lambda b, j, pt: (page_idx(b, j, pt, pi), layer, 0, 0))

    def seq_spec(r, c):
        return pl.BlockSpec((None, r, c), lambda b, j, pt: (b, 0, 0))

    grid_spec = pltpu.PrefetchScalarGridSpec(
        num_scalar_prefetch=1,
        grid=(bd, nsteps),
        in_specs=[seq_spec(nrow, ATT_WIDTH), seq_spec(n_new, ATT_WIDTH), seq_spec(n_new, ATT_WIDTH),
                  seq_spec(nrow, n_new)]
        + [kv_spec(pi) for pi in range(npp)] + [kv_spec(pi) for pi in range(npp)]
        + [lf_spec(pi) for pi in range(npp)],
        out_specs=seq_spec(t_new, ATT_WIDTH),
        scratch_shapes=[pltpu.VMEM((nrow, 1), F32), pltpu.VMEM((nrow, 1), F32),
                        pltpu.VMEM((nrow, ATT_WIDTH), F32), pltpu.VMEM((N_HEADS, 1), F32)],
    )
    return pl.pallas_call(
        functools.partial(_attn_sample_kernel, npp=npp, t_new=t_new),
        grid_spec=grid_spec,
        out_shape=jax.ShapeDtypeStruct((bd, t_new, ATT_WIDTH), F32),
        compiler_params=_params(2),
        name="attn_sample",
    )(page_table, q, kn, vn, cn, *([cache_kt] * npp), *([cache_vt] * npp), *([cache_lft] * npp))


def _mix_kernel(au_ref, av_ref, at_ref, glu_ref, gate_ref, x_ref, hist_ref,
                mixw_ref, mixb_ref, wa_ref, wb_ref, cw_ref, cb_ref, cg_ref, cbeta_ref, wc_ref, wo_ref,
                xo_ref, cstate_ref, buf_ref, ua_ref, cs_ref, *, tm, rs, hr):
    i = pl.program_id(1)

    @pl.when(i == 0)
    def _():
        buf_ref[0:hr, :] = hist_ref[...]

    buf_ref[hr:hr + tm, :] = glu_ref[...]

    off = hr - (CONV_W - 1) * rs
    rc = 64
    for r0 in range(0, tm, rc):
        acc = jnp.broadcast_to(cb_ref[...], (rc, CONV_CH))
        for j in range(CONV_W):
            acc = acc + cw_ref[j:j + 1, :] * buf_ref[off + j * rs + r0:off + j * rs + r0 + rc, :]
        y = _layer_norm(acc, cg_ref[...], cbeta_ref[...])
        cs_ref[r0:r0 + rc, :] = (y * jax.nn.sigmoid(y)).astype(BF16)
    cstate_ref[...] = buf_ref[tm:tm + hr, :]
    buf_ref[0:hr, :] = buf_ref[tm:tm + hr, :]

    lane_group = lax.broadcasted_iota(jnp.int32, (CHUNK, SGU_WIDTH), 1) // (SGU_WIDTH // SGU_GROUPS)
    for c0 in range(0, tm, CHUNK):
        vc = av_ref[c0:c0 + CHUNK, :].astype(BF16)
        mixed = _dot(mixw_ref[0], vc)
        for g in range(1, SGU_GROUPS):
            mixed = jnp.where(lane_group == g, _dot(mixw_ref[g], vc), mixed)
        ua_ref[c0:c0 + CHUNK, :] = (au_ref[c0:c0 + CHUNK, :] * (mixed + mixb_ref[...])).astype(BF16)

    ya = _dot(ua_ref[...], wa_ref[...])
    yb = _dot(at_ref[...], wb_ref[...])
    yc = _dot(cs_ref[...], wc_ref[...])
    merged = (gate_ref[:, 0:D_MODEL] * ya + gate_ref[:, D_MODEL:2 * D_MODEL] * yb
              + gate_ref[:, 2 * D_MODEL:3 * D_MODEL] * yc)
    xo_ref[...] = x_ref[...] + _dot(merged.astype(BF16), wo_ref[...])


def _mix(au, av, attn, glu, gate, x2d, hist, mixw, mixb, wa, wb, cw, cb, cg, cbeta, wc, wo, *, nseq, tm, rs):
    m = x2d.shape[0]
    nt = m // (nseq * tm)
    hr = hist.shape[1]
    row = lambda s, i: (s * nt + i, 0)
    rows = lambda c: pl.BlockSpec((tm, c), row)
    return pl.pallas_call(
        functools.partial(_mix_kernel, tm=tm, rs=rs, hr=hr),
        grid=(nseq, nt),
        in_specs=[rows(SGU_WIDTH), rows(SGU_WIDTH), rows(ATT_WIDTH), rows(CONV_CH), rows(N_BRANCH * D_MODEL),
                  rows(D_MODEL), pl.BlockSpec((None, hr, CONV_CH), lambda s, i: (s, 0, 0)),
                  _const_spec((SGU_GROUPS, CHUNK, CHUNK)), _const_spec((CHUNK, SGU_WIDTH)),
                  _const_spec((SGU_WIDTH, D_MODEL)), _const_spec((ATT_WIDTH, D_MODEL)),
                  _const_spec((CONV_W + 1, CONV_CH)), _const_spec((1, CONV_CH)), _const_spec((1, CONV_CH)),
                  _const_spec((1, CONV_CH)), _const_spec((CONV_CH, D_MODEL)), _const_spec((D_MODEL, D_MODEL))],
        out_specs=(rows(D_MODEL), pl.BlockSpec((None, hr, CONV_CH), lambda s, i: (s, 0, 0))),
        out_shape=(jax.ShapeDtypeStruct((m, D_MODEL), F32), jax.ShapeDtypeStruct((nseq, hr, CONV_CH), F32)),
        scratch_shapes=[pltpu.VMEM((hr + tm, CONV_CH), F32), pltpu.VMEM((tm, SGU_WIDTH), BF16),
                        pltpu.VMEM((tm, CONV_CH), BF16)],
        compiler_params=_params(2),
        name="mix",
    )(au, av, attn, glu, gate, x2d, hist, mixw, mixb, wa, wb, cw, cb, cg, cbeta, wc, wo)


def _ffn_kernel(x_ref, g_ref, hist_ref, wup_ref, fcw_ref, fcb_ref, wdn_ref, gf_ref,
                xo_ref, fstate_ref, h_ref, carry_ref, buf_ref, acc_ref, *, tm, rs, hr, ck, final_norm):
    i = pl.program_id(1)

    @pl.when(i == 0)
    def _():
        carry_ref[...] = hist_ref[...]

    x = x_ref[...]
    h_ref[...] = _rms(x, g_ref[...]).astype(BF16)
    acc_ref[...] = x
    for c in range(D_FF // ck):
        halves = []
        for half in range(2):
            col = half * D_FF + c * ck
            slot = 2 * (c % 2) + half
            buf_ref[slot, 0:hr, :] = carry_ref[:, col:col + ck]
            buf_ref[slot, hr:hr + tm, :] = _dot(h_ref[...], wup_ref[:, col:col + ck])
            carry_ref[:, col:col + ck] = buf_ref[slot, tm:tm + hr, :]
            conv = fcb_ref[:, col:col + ck]
            for j in range(FFN_CONV_W):
                o = hr - (FFN_CONV_W - 1 - j) * rs
                conv = conv + fcw_ref[j:j + 1, col:col + ck] * buf_ref[slot, o:o + tm, :]
            halves.append(conv)
        act = (jax.nn.gelu(halves[0]) * halves[1]).astype(BF16)
        acc_ref[...] += _dot(act, wdn_ref[c * ck:(c + 1) * ck, :])
    fstate_ref[...] = carry_ref[...]
    out = acc_ref[...]
    if final_norm:
        out = _rms(out, gf_ref[...])
    xo_ref[...] = out


def _ffn(x2d, g, hist, wup, fcw, fcb, wdn, gf, *, nseq, tm, rs, final_norm):
    m = x2d.shape[0]
    nt = m // (nseq * tm)
    hr = hist.shape[1]
    ck = 256
    row = lambda s, i: (s * nt + i, 0)
    rows = lambda c: pl.BlockSpec((tm, c), row)
    state_spec = pl.BlockSpec((None, hr, 2 * D_FF), lambda s, i: (s, 0, 0))
    return pl.pallas_call(
        functools.partial(_ffn_kernel, tm=tm, rs=rs, hr=hr, ck=ck, final_norm=final_norm),
        grid=(nseq, nt),
        in_specs=[rows(D_MODEL), _const_spec((1, D_MODEL)), state_spec,
                  _const_spec((D_MODEL, 2 * D_FF)), _const_spec((SUBLANES, 2 * D_FF)),
                  _const_spec((1, 2 * D_FF)), _const_spec((D_FF, D_MODEL)), _const_spec((1, D_MODEL))],
        out_specs=(rows(D_MODEL), state_spec),
        out_shape=(jax.ShapeDtypeStruct((m, D_MODEL), F32), jax.ShapeDtypeStruct((nseq, hr, 2 * D_FF), F32)),
        scratch_shapes=[pltpu.VMEM((tm, D_MODEL), BF16), pltpu.VMEM((hr, 2 * D_FF), F32),
                        pltpu.VMEM((4, hr + tm, ck), F32), pltpu.VMEM((tm, D_MODEL), F32)],
        compiler_params=_params(2),
        name="ffn",
    )(x2d, g, hist, wup, fcw, fcb, wdn, gf)


def _round_up(n, k):
    return -(-n // k) * k


def _layer_weights(l, norm1_g, w_in, b_forget, sgu_ln_g, sgu_ln_b, sgu_w, sgu_b, w_proj_a, w_proj_b,
                   conv_w, conv_b, conv_ln_g, conv_ln_b, w_proj_c, w_out, norm2_g, w_up,
                   ffn_conv_w, ffn_conv_b, w_down):
    w = w_in[l]
    scale = HEAD_DIM ** -0.5
    w_r = jnp.concatenate([
        w[:, :R_Q], w[:, R_Q:R_K] * scale, w[:, R_K:OFF_F], w[:, OFF_GLU:IN_COLS],
        w[:, OFF_F:OFF_GLU], jnp.zeros((D_MODEL, LANES - N_HEADS), F32)], axis=1).astype(BF16)
    bf = jnp.pad(b_forget[l], (0, LANES - N_HEADS))[None, :]
    causal = jnp.tril(jnp.ones((CHUNK, CHUNK), dtype=bool))
    return dict(
        g1=norm1_g[l][None, :], w_in=w_r, bf=bf, lng=sgu_ln_g[l][None, :], lnb=sgu_ln_b[l][None, :],
        sgu_w=jnp.where(causal, sgu_w[l], 0.0), sgu_b=sgu_b[l],
        wa=w_proj_a[l].astype(BF16), wb=w_proj_b[l].astype(BF16), wc=w_proj_c[l].astype(BF16),
        cw=jnp.pad(conv_w[l], ((0, 1), (0, 0))), cb=conv_b[l][None, :],
        cg=conv_ln_g[l][None, :], cbeta=conv_ln_b[l][None, :],
        wo=w_out[l].astype(BF16), g2=norm2_g[l][None, :], wup=w_up[l].astype(BF16),
        fcw=jnp.pad(ffn_conv_w[l], ((0, SUBLANES - FFN_CONV_W), (0, 0))), fcb=ffn_conv_b[l][None, :],
        wdn=w_down[l].astype(BF16))


def _mix_bias(sgu_b_rows):
    return jnp.repeat(sgu_b_rows.T, SGU_WIDTH // SGU_GROUPS, axis=1)


def kernel(x_prompt, x_sample, cache_k, cache_v, cache_logf, state_conv, state_ffn, page_table, norm1_g, w_in, b_forget, sgu_ln_g, sgu_ln_b, sgu_w, sgu_b, w_proj_a, w_proj_b, conv_w, conv_b, conv_ln_g, conv_ln_b, w_proj_c, w_out, norm2_g, w_up, ffn_conv_w, ffn_conv_b, w_down, norm_f_g):
    bp, seq, _ = x_prompt.shape
    bd, t_new, _ = x_sample.shape
    depth = w_in.shape[0]
    assert seq % CHUNK == 0 and PAGE_SIZE == CHUNK and t_new <= CHUNK
    assert seq >= CONV_W - 1 and t_new >= FFN_CONV_W - 1 and t_new <= CONV_W - 1

    tm_in, tm_mix, tm_ffn = 512, 512, 1024
    tq = 512
    attn_pairs = 2
    ms = bd * t_new
    assert ms == CHUNK and page_table.shape[1] % PAGES_PER_STEP == 0

    hr_conv_p = _round_up(CONV_W - 1, SUBLANES)
    hr_ffn_p = _round_up(FFN_CONV_W - 1, SUBLANES)
    hr_conv_s = (CONV_W - 1) * bd
    hr_ffn_s = (FFN_CONV_W - 1) * bd

    tri_p = jnp.tril(jnp.ones((tm_in, tm_in), F32)).astype(BF16)
    eye_b = jnp.eye(bd, dtype=F32)
    tri_s = jnp.kron(jnp.tril(jnp.ones((t_new, t_new), F32)), eye_b).astype(BF16)
    cache_kt = jnp.transpose(cache_k, (0, 1, 3, 4, 2))
    cache_vt = jnp.transpose(cache_v, (0, 1, 3, 4, 2))
    cache_lft = jnp.transpose(cache_logf, (0, 1, 3, 2))
    head_rows = (jnp.arange(t_new * N_HEADS)[:, None] % N_HEADS) == (jnp.arange(ATT_WIDTH)[None, :] // HEAD_DIM)
    n_new = _round_up(t_new, SUBLANES)
    gf = norm_f_g[None, :]

    xp = x_prompt.reshape(bp * seq, D_MODEL)
    xs = jnp.transpose(x_sample, (1, 0, 2)).reshape(ms, D_MODEL)

    def to_seq_major(a):
        return jnp.transpose(a.reshape(t_new, bd, a.shape[-1]), (1, 0, 2))

    outs = {k: [] for k in ("fp", "cp", "ffp", "fs", "avs", "cs", "ffs")}
    kt_p = jnp.zeros((bp, depth, ATT_WIDTH, seq), F32)
    vt_p = jnp.zeros((bp, depth, ATT_WIDTH, seq), F32)
    kt_s = jnp.zeros((1, depth, ATT_WIDTH, ms), F32)
    vt_s = jnp.zeros((1, depth, ATT_WIDTH, ms), F32)
    for l in range(depth):
        lw = _layer_weights(l, norm1_g, w_in, b_forget, sgu_ln_g, sgu_ln_b, sgu_w, sgu_b, w_proj_a, w_proj_b,
                            conv_w, conv_b, conv_ln_g, conv_ln_b, w_proj_c, w_out, norm2_g, w_up,
                            ffn_conv_w, ffn_conv_b, w_down)
        last = l == depth - 1

        au, av, qx, kx, lf, c, glu, gate, kt_p, vt_p = _inproj(
            xp, lw["g1"], lw["w_in"], lw["bf"], lw["lng"], lw["lnb"], tri_p, kt_p, vt_p, l, nseq=bp, tm=tm_in)
        attn = _attn_prompt(qx.reshape(bp, seq, QKX_WIDTH), kx.reshape(bp, seq, QKX_WIDTH), vt_p, l,
                            tq=tq, pairs=attn_pairs)
        xp, cstate = _mix(au, av, attn.reshape(bp * seq, ATT_WIDTH), glu, gate, xp,
                          jnp.zeros((bp, hr_conv_p, CONV_CH), F32),
                          lw["sgu_w"].astype(BF16), _mix_bias(lw["sgu_b"]),
                          lw["wa"], lw["wb"], lw["cw"], lw["cb"], lw["cg"], lw["cbeta"], lw["wc"], lw["wo"],
                          nseq=bp, tm=tm_mix, rs=1)
        xp, fstate = _ffn(xp, lw["g2"], jnp.zeros((bp, hr_ffn_p, 2 * D_FF), F32), lw["wup"], lw["fcw"],
                          lw["fcb"], lw["wdn"], gf, nseq=bp, tm=tm_ffn, rs=1, final_norm=last)
        outs["fp"].append(lf.reshape(bp, seq, N_HEADS))
        outs["cp"].append(cstate[:, hr_conv_p - (CONV_W - 1):])
        outs["ffp"].append(fstate[:, hr_ffn_p - (FFN_CONV_W - 1):])

        au, av, qx, kx, lf, c, glu, gate, kt_s, vt_s = _inproj(
            xs, lw["g1"], lw["w_in"], lw["bf"], lw["lng"], lw["lnb"], tri_s, kt_s, vt_s, l, nseq=1, tm=ms)
        k, v = kt_s[0, l].T, vt_s[0, l].T
        q = qx.reshape(ms, N_HEADS // 2, PAIR_X)[:, :, :LANES].reshape(ms, ATT_WIDTH)
        q_bd = jnp.where(head_rows, jnp.repeat(to_seq_major(q), N_HEADS, axis=1), jnp.zeros((), BF16))
        pad_new = lambda a: jnp.pad(to_seq_major(a).astype(BF16), ((0, 0), (0, n_new - t_new), (0, 0)))
        cn = jnp.tile(jnp.transpose(to_seq_major(c), (0, 2, 1)), (1, t_new, 1))
        cn = jnp.pad(cn, ((0, 0), (0, 0), (0, n_new - t_new)))
        attn = _attn_sample(page_table, l, q_bd, pad_new(k), pad_new(v), cn,
                            cache_kt, cache_vt, cache_lft, npp=PAGES_PER_STEP)
        attn = jnp.transpose(attn, (1, 0, 2)).reshape(ms, ATT_WIDTH).astype(BF16)
        mixw_s = jnp.stack([jnp.kron(lw["sgu_w"][g, :t_new, :t_new], eye_b) for g in range(SGU_GROUPS)])
        mixb_s = _mix_bias(jnp.repeat(lw["sgu_b"][:, :t_new], bd, axis=1))
        hist_c = jnp.transpose(state_conv[:, l], (1, 0, 2)).reshape(1, hr_conv_s, CONV_CH)
        xs, cstate = _mix(au, av, attn, glu, gate, xs, hist_c, mixw_s.astype(BF16), mixb_s,
                          lw["wa"], lw["wb"], lw["cw"], lw["cb"], lw["cg"], lw["cbeta"], lw["wc"], lw["wo"],
                          nseq=1, tm=ms, rs=bd)
        hist_f = jnp.transpose(state_ffn[:, l], (1, 0, 2)).reshape(1, hr_ffn_s, 2 * D_FF)
        xs, fstate = _ffn(xs, lw["g2"], hist_f, lw["wup"], lw["fcw"], lw["fcb"], lw["wdn"], gf,
                          nseq=1, tm=ms, rs=bd, final_norm=last)
        outs["fs"].append(to_seq_major(lf))
        outs["avs"].append(to_seq_major(av))
        outs["cs"].append(jnp.transpose(cstate.reshape(CONV_W - 1, bd, CONV_CH), (1, 0, 2)))
        outs["ffs"].append(jnp.transpose(fstate.reshape(FFN_CONV_W - 1, bd, 2 * D_FF), (1, 0, 2)))

    y_prompt = xp.reshape(bp, seq, D_MODEL)
    y_sample = to_seq_major(xs)
    st = lambda key: jnp.stack(outs[key], axis=1)

    def heads_last_p(a):
        return jnp.transpose(a.reshape(bp, depth, N_HEADS, HEAD_DIM, seq), (0, 1, 4, 2, 3))

    def heads_last_s(a):
        return jnp.transpose(a.reshape(depth, N_HEADS, HEAD_DIM, t_new, bd), (4, 0, 3, 1, 2))

    return (y_prompt, y_sample, heads_last_p(kt_p), heads_last_p(vt_p), st("fp"), st("cp"), st("ffp"),
            heads_last_s(kt_s), heads_last_s(vt_s), st("fs"), st("avs"), st("cs"), st("ffs"))
```

```python
import functools

import jax
import jax.numpy as jnp
from jax import lax
from jax.experimental import pallas as pl
from jax.experimental.pallas import tpu as pltpu

F32 = jnp.float32
BF16 = jnp.bfloat16

D_MODEL = 1024
SGU_WIDTH = 256
SGU_GROUPS = 4
CHUNK = 128
N_HEADS = 8
HEAD_DIM = 64
ATT_WIDTH = N_HEADS * HEAD_DIM
CONV_CH = 256
CONV_W = 31
D_FF = 2816
FFN_CONV_W = 3
N_BRANCH = 3
PAGE_SIZE = 128
EPS = 1e-6
NEG_INF = -1e30
LOG2E = 1.4426950408889634

OFF_F = 2 * SGU_WIDTH + 3 * ATT_WIDTH
OFF_GLU = OFF_F + N_HEADS
IN_COLS = OFF_GLU + 2 * CONV_CH + N_BRANCH * D_MODEL

LANES = 128
SUBLANES = 8
R_Q = 2 * SGU_WIDTH
R_K = R_Q + ATT_WIDTH
R_V = R_K + ATT_WIDTH
R_GLU = R_V + ATT_WIDTH
R_GATE = R_GLU + 2 * CONV_CH
R_F = R_GATE + N_BRANCH * D_MODEL
R_COLS = R_F + LANES

PAIR_X = 2 * LANES
QKX_WIDTH = (N_HEADS // 2) * PAIR_X
AUG_LANES = 6 * N_HEADS

VMEM_LIMIT_BYTES = 56 * 1024 * 1024
PAGES_PER_STEP = 16


def _dot(a, b):
    return jnp.dot(a, b, preferred_element_type=F32)


def _dot_nt(a, b):
    return lax.dot_general(a, b, (((1,), (1,)), ((), ())), preferred_element_type=F32)


def _split3(x):
    hi = x.astype(BF16)
    r1 = x - hi.astype(F32)
    mid = r1.astype(BF16)
    lo = (r1 - mid.astype(F32)).astype(BF16)
    return hi, mid, lo


def _dot01(m01, x):
    hi, mid, lo = _split3(x)
    return _dot(m01, hi) + _dot(m01, mid) + _dot(m01, lo)


def _dot01_right(x, m01):
    hi, mid, lo = _split3(x)
    return _dot(hi, m01) + _dot(mid, m01) + _dot(lo, m01)


def _rms(x, g):
    return x * lax.rsqrt(jnp.mean(x * x, axis=-1, keepdims=True) + EPS) * g


def _layer_norm(x, g, b):
    mu = jnp.mean(x, axis=-1, keepdims=True)
    xc = x - mu
    var = jnp.mean(xc * xc, axis=-1, keepdims=True)
    return xc * lax.rsqrt(var + EPS) * g + b


def _log_sigmoid(x):
    return jnp.minimum(x, 0.0) - jnp.log1p(jnp.exp(-jnp.abs(x)))


def _const_spec(shape):
    nd = len(shape)
    return pl.BlockSpec(shape, lambda *_: (0,) * nd, pipeline_mode=pl.Buffered(1))


def _params(n_axes):
    return pltpu.CompilerParams(dimension_semantics=("arbitrary",) * n_axes,
                                vmem_limit_bytes=VMEM_LIMIT_BYTES)


def _inproj_kernel(x_ref, g_ref, w_ref, bf_ref, lng_ref, lnb_ref, tri_ref, kt_in_ref, vt_in_ref,
                   au_ref, av_ref, qx_ref, kx_ref, lf_ref, c_ref, glu_ref, gate_ref, kt_ref, vt_ref,
                   h_ref, carry_ref, *, tm):
    del kt_in_ref, vt_in_ref
    i = pl.program_id(1)
    h_ref[...] = _rms(x_ref[...], g_ref[...]).astype(BF16)

    def seg(c0, n):
        return _dot(h_ref[...], w_ref[:, c0:c0 + n])

    au_ref[...] = jax.nn.gelu(seg(0, SGU_WIDTH))
    av_ref[...] = _layer_norm(jax.nn.gelu(seg(SGU_WIDTH, SGU_WIDTH)), lng_ref[...], lnb_ref[...])
    qq = seg(R_Q, ATT_WIDTH).astype(BF16)
    kk = seg(R_K, ATT_WIDTH)
    kt_ref[...] = kk.T
    kkb = kk.astype(BF16)
    for p in range(N_HEADS // 2):
        qx_ref[:, p * PAIR_X:p * PAIR_X + LANES] = qq[:, p * LANES:(p + 1) * LANES]
        kx_ref[:, p * PAIR_X:p * PAIR_X + LANES] = kkb[:, p * LANES:(p + 1) * LANES]
    vt_ref[...] = seg(R_V, ATT_WIDTH).T
    zg = seg(R_GLU, 2 * CONV_CH)
    glu_ref[...] = zg[:, :CONV_CH] * jax.nn.sigmoid(zg[:, CONV_CH:])
    gate_chunk = 512
    for c in range(N_BRANCH * D_MODEL // gate_chunk):
        gate_ref[:, c * gate_chunk:(c + 1) * gate_chunk] = jax.nn.sigmoid(seg(R_GATE + c * gate_chunk, gate_chunk))
    lf = _log_sigmoid(seg(R_F, LANES) + bf_ref[...])
    lf_ref[...] = lf[:, :N_HEADS]

    @pl.when(i == 0)
    def _():
        carry_ref[...] = jnp.zeros_like(carry_ref)

    c = _dot01(tri_ref[...], lf) + carry_ref[...]
    c_ref[...] = c[:, :N_HEADS]
    carry_ref[...] = c[tm - 1:tm, :]

    lane = lax.broadcasted_iota(jnp.int32, (1, LANES), 1)
    hi, mid, lo = (part.astype(F32) for part in _split3(jnp.where(lane < N_HEADS, c * LOG2E, 0.0)))
    kaug = (jnp.where((lane >= 3 * N_HEADS) & (lane < AUG_LANES), 1.0, 0.0)
            - (hi + pltpu.roll(mid, N_HEADS, 1) + pltpu.roll(lo, 2 * N_HEADS, 1))).astype(BF16)
    qaug = (jnp.where(lane < 3 * N_HEADS, 1.0, 0.0) + pltpu.roll(hi, 3 * N_HEADS, 1)
            + pltpu.roll(mid, 4 * N_HEADS, 1) + pltpu.roll(lo, 5 * N_HEADS, 1)).astype(BF16)
    for p in range(N_HEADS // 2):
        qx_ref[:, p * PAIR_X + LANES:(p + 1) * PAIR_X] = qaug
        kx_ref[:, p * PAIR_X + LANES:(p + 1) * PAIR_X] = kaug


def _inproj(x2d, g, w, bf, lng, lnb, tri, kt_all, vt_all, layer, *, nseq, tm):
    m = x2d.shape[0]
    nt = m // (nseq * tm)
    row = lambda s, i: (s * nt + i, 0)

    def rows(c):
        return pl.BlockSpec((tm, c), row)

    out_shape = (
        jax.ShapeDtypeStruct((m, SGU_WIDTH), F32), jax.ShapeDtypeStruct((m, SGU_WIDTH), F32),
        jax.ShapeDtypeStruct((m, QKX_WIDTH), BF16), jax.ShapeDtypeStruct((m, QKX_WIDTH), BF16),
        jax.ShapeDtypeStruct((m, N_HEADS), F32), jax.ShapeDtypeStruct((m, N_HEADS), F32),
        jax.ShapeDtypeStruct((m, CONV_CH), F32), jax.ShapeDtypeStruct((m, N_BRANCH * D_MODEL), F32),
        jax.ShapeDtypeStruct(kt_all.shape, F32), jax.ShapeDtypeStruct(vt_all.shape, F32),
    )
    slab = pl.BlockSpec((None, None, ATT_WIDTH, tm), lambda s, i: (s, layer, 0, i))
    out_specs = (rows(SGU_WIDTH), rows(SGU_WIDTH), rows(QKX_WIDTH), rows(QKX_WIDTH), rows(N_HEADS),
                 rows(N_HEADS), rows(CONV_CH), rows(N_BRANCH * D_MODEL), slab, slab)
    any_spec = pl.BlockSpec(memory_space=pl.ANY)
    return pl.pallas_call(
        functools.partial(_inproj_kernel, tm=tm),
        grid=(nseq, nt),
        in_specs=[rows(D_MODEL), _const_spec((1, D_MODEL)), _const_spec((D_MODEL, R_COLS)),
                  _const_spec((1, LANES)), _const_spec((1, SGU_WIDTH)), _const_spec((1, SGU_WIDTH)),
                  _const_spec((tm, tm)), any_spec, any_spec],
        out_specs=out_specs,
        out_shape=out_shape,
        input_output_aliases={7: 8, 8: 9},
        scratch_shapes=[pltpu.VMEM((tm, D_MODEL), BF16), pltpu.VMEM((1, LANES), F32)],
        compiler_params=_params(2),
        name="inproj",
    )(x2d, g, w, bf, lng, lnb, tri, kt_all, vt_all)


def _attn_prompt_kernel(qx_ref, kx_ref, vt_ref, o_ref, qm_ref, m_ref, l_ref, acc_ref, s_ref, *, tq, pairs):
    grp = pl.program_id(1)
    i = pl.program_id(2)
    nh = 2 * pairs
    lane_x = lax.broadcasted_iota(jnp.int32, (1, PAIR_X), 1)
    key_ids = lax.broadcasted_iota(jnp.int32, (tq, tq), 0)
    qry_ids = lax.broadcasted_iota(jnp.int32, (tq, tq), 1)
    for hl in range(nh):
        pr, hh = divmod(hl, 2)
        head = grp * nh + hl
        own = ((lane_x // HEAD_DIM) == hh) | (
            (lane_x >= LANES) & (lane_x < LANES + AUG_LANES) & ((lane_x - LANES) % N_HEADS == head))
        qm_ref[hl] = qx_ref[:, pr * PAIR_X:(pr + 1) * PAIR_X] * jnp.where(own, 1.0, 0.0).astype(BF16)
    m_ref[...] = jnp.full_like(m_ref, NEG_INF)
    l_ref[...] = jnp.zeros_like(l_ref)
    acc_ref[...] = jnp.zeros_like(acc_ref)

    def scores(start, hl):
        pr = hl // 2
        return _dot_nt(kx_ref[pl.ds(start, tq), pr * PAIR_X:(pr + 1) * PAIR_X], qm_ref[hl])

    s_ref[0] = scores(0, 0)

    def step(j, diagonal):
        start = pl.multiple_of(j * tq, tq)
        for hl in range(nh):
            if hl + 1 < nh:
                s_ref[(hl + 1) % 2] = scores(start, hl + 1)
            elif not diagonal:
                s_ref[0] = scores(pl.multiple_of(start + tq, tq), 0)
            st = s_ref[hl % 2]
            if diagonal:
                st = jnp.where(key_ids <= qry_ids, st, NEG_INF)
            m_old = m_ref[hl]
            m_new = jnp.maximum(m_old, jnp.max(st, axis=0, keepdims=True))
            alpha = jnp.exp2(m_old - m_new)
            p = jnp.exp2(st - m_new)
            l_ref[hl] = alpha * l_ref[hl] + jnp.sum(p, axis=0, keepdims=True)
            vt = vt_ref[hl * HEAD_DIM:(hl + 1) * HEAD_DIM, pl.ds(start, tq)].astype(BF16)
            acc_ref[hl] = alpha * acc_ref[hl] + _dot(vt, p.astype(BF16))
            m_ref[hl] = m_new

    def body(j, carry):
        step(j, False)
        return carry

    lax.fori_loop(0, i, body, 0)
    step(i, True)
    for pr in range(pairs):
        out_t = jnp.concatenate([acc_ref[2 * pr] / l_ref[2 * pr], acc_ref[2 * pr + 1] / l_ref[2 * pr + 1]], axis=0)
        o_ref[:, pr * LANES:(pr + 1) * LANES] = out_t.T.astype(o_ref.dtype)


def _attn_prompt(qx, kx, vt_all, layer, *, tq, pairs):
    b, s, _ = qx.shape
    ngrp = N_HEADS // (2 * pairs)
    nh = 2 * pairs
    return pl.pallas_call(
        functools.partial(_attn_prompt_kernel, tq=tq, pairs=pairs),
        grid=(b, ngrp, s // tq),
        in_specs=[
            pl.BlockSpec((None, tq, pairs * PAIR_X), lambda bb, g, i: (bb, i, g)),
            pl.BlockSpec((None, s, pairs * PAIR_X), lambda bb, g, i: (bb, 0, g)),
            pl.BlockSpec((None, None, pairs * LANES, s), lambda bb, g, i: (bb, layer, g, 0)),
        ],
        out_specs=pl.BlockSpec((None, tq, pairs * LANES), lambda bb, g, i: (bb, i, g)),
        out_shape=jax.ShapeDtypeStruct((b, s, ATT_WIDTH), BF16),
        scratch_shapes=[pltpu.VMEM((nh, tq, PAIR_X), BF16), pltpu.VMEM((nh, 1, tq), F32),
                        pltpu.VMEM((nh, 1, tq), F32), pltpu.VMEM((nh, HEAD_DIM, tq), F32),
                        pltpu.VMEM((2, tq, tq), F32)],
        compiler_params=_params(3),
        name="attn_prompt",
    )(qx, kx, vt_all)


def _attn_sample_kernel(pt_ref, q_ref, kn_ref, vn_ref, cn_ref, *rest, npp, t_new):
    del pt_ref
    k_refs = rest[:npp]
    v_refs = rest[npp:2 * npp]
    lf_refs = rest[2 * npp:3 * npp]
    o_ref = rest[3 * npp]
    m_ref, l_ref, acc_ref, run_ref = rest[3 * npp + 1:]
    j = pl.program_id(1)
    nrow = t_new * N_HEADS

    @pl.when(j == 0)
    def _():
        m_ref[...] = jnp.full_like(m_ref, NEG_INF)
        l_ref[...] = jnp.zeros_like(l_ref)
        acc_ref[...] = jnp.zeros_like(acc_ref)
        run_ref[...] = jnp.zeros_like(run_ref)

    q = q_ref[...]

    def online_update(s, pv):
        m_old = m_ref[...]
        m_new = jnp.maximum(m_old, jnp.max(s, axis=-1, keepdims=True))
        alpha = jnp.exp2(m_old - m_new)
        p = jnp.exp2(s - m_new)
        l_ref[...] = alpha * l_ref[...] + jnp.sum(p, axis=-1, keepdims=True)
        acc_ref[...] = alpha * acc_ref[...] + pv(p.astype(BF16))
        m_ref[...] = m_new

    lf_all = jnp.concatenate([lf_refs[pi][...] for pi in range(npp)], axis=0) * LOG2E
    later = (lax.broadcasted_iota(jnp.int32, (PAGE_SIZE, PAGE_SIZE), 0)
             > lax.broadcasted_iota(jnp.int32, (PAGE_SIZE, PAGE_SIZE), 1))
    suf_all = _dot01_right(lf_all, jnp.where(later, 1.0, 0.0).astype(BF16))
    tot_all = suf_all[:, 0:1] + lf_all[:, 0:1]
    run = run_ref[...]
    pieces = []
    for pi in range(npp):
        r = suf_all[pi * N_HEADS:(pi + 1) * N_HEADS] + run
        run = run + tot_all[pi * N_HEADS:(pi + 1) * N_HEADS]
        kp = k_refs[pi][...].reshape(ATT_WIDTH, PAGE_SIZE).astype(BF16)
        pieces.append(_dot(q, kp) + jnp.concatenate([r] * t_new, axis=0))
    run_ref[...] = run

    def pv_pages(p):
        out = None
        for pi in range(npp):
            vp = v_refs[pi][...].reshape(ATT_WIDTH, PAGE_SIZE).astype(BF16)
            term = _dot_nt(p[:, pi * PAGE_SIZE:(pi + 1) * PAGE_SIZE], vp)
            out = term if out is None else out + term
        return out

    online_update(jnp.concatenate(pieces, axis=-1), pv_pages)

    @pl.when(j == pl.num_programs(1) - 1)
    def _():
        ln = _dot_nt(q, kn_ref[...]) - cn_ref[...] * LOG2E
        rr = lax.broadcasted_iota(jnp.int32, ln.shape, 0)
        cc = lax.broadcasted_iota(jnp.int32, ln.shape, 1)
        online_update(jnp.where(cc <= rr // N_HEADS, ln, NEG_INF), lambda p: _dot(p, vn_ref[...]))
        full = acc_ref[...] / l_ref[...]
        row_head = lax.broadcasted_iota(jnp.int32, full.shape, 0) % N_HEADS
        lane_head = lax.broadcasted_iota(jnp.int32, full.shape, 1) // HEAD_DIM
        full = jnp.where(row_head == lane_head, full, 0.0)
        for t in range(t_new):
            o_ref[t:t + 1, :] = jnp.sum(full[t * N_HEADS:(t + 1) * N_HEADS], axis=0, keepdims=True)


def _attn_sample(page_table, layer, q, kn, vn, cn, cache_kt, cache_vt, cache_lft, *, npp):
    bd, nrow, _ = q.shape
    n_pages = page_table.shape[1]
    nsteps = n_pages // npp
    t_new = nrow // N_HEADS
    n_new = kn.shape[1]

    def page_idx(b, j, pt, pi):
        return pt[b, n_pages - 1 - (j * npp + pi)]

    def kv_spec(pi):
        return pl.BlockSpec((None, None, N_HEADS, HEAD_DIM, PAGE_SIZE),
                            lambda b, j, pt: (page_idx(b, j, pt, pi), layer, 0, 0, 0))

    def lf_spec(pi):
        return pl.BlockSpec((None, None, N_HEADS, PAGE_SIZE),
                            lambda b, j, pt: (page_idx(b, j, pt, pi), layer, 0, 0))

    def seq_spec(r, c):
        return pl.BlockSpec((None, r, c), lambda b, j, pt: (b, 0, 0))

    grid_spec = pltpu.PrefetchScalarGridSpec(
        num_scalar_prefetch=1,
        grid=(bd, nsteps),
        in_specs=[seq_spec(nrow, ATT_WIDTH), seq_spec(n_new, ATT_WIDTH), seq_spec(n_new, ATT_WIDTH),
                  seq_spec(nrow, n_new)]
        + [kv_spec(pi) for pi in range(npp)] + [kv_spec(pi) for pi in range(npp)]
        + [lf_spec(pi) for pi in range(npp)],
        out_specs=seq_spec(t_new, ATT_WIDTH),
        scratch_shapes=[pltpu.VMEM((nrow, 1), F32), pltpu.VMEM((nrow, 1), F32),
                        pltpu.VMEM((nrow, ATT_WIDTH), F32), pltpu.VMEM((N_HEADS, 1), F32)],
    )
    return pl.pallas_call(
        functools.partial(_attn_sample_kernel, npp=npp, t_new=t_new),
        grid_spec=grid_spec,
        out_shape=jax.ShapeDtypeStruct((bd, t_new, ATT_WIDTH), F32),
        compiler_params=_params(2),
        name="attn_sample",
    )(page_table, q, kn, vn, cn, *([cache_kt] * npp), *([cache_vt] * npp), *([cache_lft] * npp))


def _mix_kernel(au_ref, av_ref, at_ref, glu_ref, gate_ref, x_ref, hist_ref,
                mixw_ref, mixb_ref, wa_ref, wb_ref, cw_ref, cb_ref, cg_ref, cbeta_ref, wc_ref, wo_ref,
                xo_ref, cstate_ref, buf_ref, ua_ref, cs_ref, *, tm, rs, hr):
    i = pl.program_id(1)

    @pl.when(i == 0)
    def _():
        buf_ref[0:hr, :] = hist_ref[...]

    buf_ref[hr:hr + tm, :] = glu_ref[...]

    off = hr - (CONV_W - 1) * rs
    rc = 64
    for r0 in range(0, tm, rc):
        acc = jnp.broadcast_to(cb_ref[...], (rc, CONV_CH))
        for j in range(CONV_W):
            acc = acc + cw_ref[j:j + 1, :] * buf_ref[off + j * rs + r0:off + j * rs + r0 + rc, :]
        y = _layer_norm(acc, cg_ref[...], cbeta_ref[...])
        cs_ref[r0:r0 + rc, :] = (y * jax.nn.sigmoid(y)).astype(BF16)
    cstate_ref[...] = buf_ref[tm:tm + hr, :]
    buf_ref[0:hr, :] = buf_ref[tm:tm + hr, :]

    lane_group = lax.broadcasted_iota(jnp.int32, (CHUNK, SGU_WIDTH), 1) // (SGU_WIDTH // SGU_GROUPS)
    for c0 in range(0, tm, CHUNK):
        vc = av_ref[c0:c0 + CHUNK, :].astype(BF16)
        mixed = _dot(mixw_ref[0], vc)
        for g in range(1, SGU_GROUPS):
            mixed = jnp.where(lane_group == g, _dot(mixw_ref[g], vc), mixed)
        ua_ref[c0:c0 + CHUNK, :] = (au_ref[c0:c0 + CHUNK, :] * (mixed + mixb_ref[...])).astype(BF16)

    ya = _dot(ua_ref[...], wa_ref[...])
    yb = _dot(at_ref[...], wb_ref[...])
    yc = _dot(cs_ref[...], wc_ref[...])
    merged = (gate_ref[:, 0:D_MODEL] * ya + gate_ref[:, D_MODEL:2 * D_MODEL] * yb
              + gate_ref[:, 2 * D_MODEL:3 * D_MODEL] * yc)
    xo_ref[...] = x_ref[...] + _dot(merged.astype(BF16), wo_ref[...])


def _mix(au, av, attn, glu, gate, x2d, hist, mixw, mixb, wa, wb, cw, cb, cg, cbeta, wc, wo, *, nseq, tm, rs):
    m = x2d.shape[0]
    nt = m // (nseq * tm)
    hr = hist.shape[1]
    row = lambda s, i: (s * nt + i, 0)
    rows = lambda c: pl.BlockSpec((tm, c), row)
    return pl.pallas_call(
        functools.partial(_mix_kernel, tm=tm, rs=rs, hr=hr),
        grid=(nseq, nt),
        in_specs=[rows(SGU_WIDTH), rows(SGU_WIDTH), rows(ATT_WIDTH), rows(CONV_CH), rows(N_BRANCH * D_MODEL),
                  rows(D_MODEL), pl.BlockSpec((None, hr, CONV_CH), lambda s, i: (s, 0, 0)),
                  _const_spec((SGU_GROUPS, CHUNK, CHUNK)), _const_spec((CHUNK, SGU_WIDTH)),
                  _const_spec((SGU_WIDTH, D_MODEL)), _const_spec((ATT_WIDTH, D_MODEL)),
                  _const_spec((CONV_W + 1, CONV_CH)), _const_spec((1, CONV_CH)), _const_spec((1, CONV_CH)),
                  _const_spec((1, CONV_CH)), _const_spec((CONV_CH, D_MODEL)), _const_spec((D_MODEL, D_MODEL))],
        out_specs=(rows(D_MODEL), pl.BlockSpec((None, hr, CONV_CH), lambda s, i: (s, 0, 0))),
        out_shape=(jax.ShapeDtypeStruct((m, D_MODEL), F32), jax.ShapeDtypeStruct((nseq, hr, CONV_CH), F32)),
        scratch_shapes=[pltpu.VMEM((hr + tm, CONV_CH), F32), pltpu.VMEM((tm, SGU_WIDTH), BF16),
                        pltpu.VMEM((tm, CONV_CH), BF16)],
        compiler_params=_params(2),
        name="mix",
    )(au, av, attn, glu, gate, x2d, hist, mixw, mixb, wa, wb, cw, cb, cg, cbeta, wc, wo)


def _ffn_kernel(x_ref, g_ref, hist_ref, wup_ref, fcw_ref, fcb_ref, wdn_ref, gf_ref,
                xo_ref, fstate_ref, h_ref, carry_ref, acc_ref, *buf_refs, tm, rs, hr, ck, final_norm):
    i = pl.program_id(1)

    @pl.when(i == 0)
    def _():
        carry_ref[...] = hist_ref[...]

    x = x_ref[...]
    h_ref[...] = _rms(x, g_ref[...]).astype(BF16)
    acc_ref[...] = x
    nck = D_FF // ck

    def up_project(c):
        for half in range(2):
            col = half * D_FF + c * ck
            slot = 2 * (c % 2) + half
            buf_refs[slot][0:hr, :] = carry_ref[:, col:col + ck]
            buf_refs[slot][hr:hr + tm, :] = _dot(h_ref[...], wup_ref[:, col:col + ck])
            carry_ref[:, col:col + ck] = buf_refs[slot][tm:tm + hr, :]

    up_project(0)
    for c in range(nck):
        if c + 1 < nck:
            up_project(c + 1)
        halves = []
        for half in range(2):
            col = half * D_FF + c * ck
            slot = 2 * (c % 2) + half
            conv = fcb_ref[:, col:col + ck]
            for j in range(FFN_CONV_W):
                o = hr - (FFN_CONV_W - 1 - j) * rs
                conv = conv + fcw_ref[j:j + 1, col:col + ck] * buf_refs[slot][o:o + tm, :]
            halves.append(conv)
        act = (jax.nn.gelu(halves[0]) * halves[1]).astype(BF16)
        acc_ref[...] += _dot(act, wdn_ref[c * ck:(c + 1) * ck, :])
    fstate_ref[...] = carry_ref[...]
    out = acc_ref[...]
    if final_norm:
        out = _rms(out, gf_ref[...])
    xo_ref[...] = out


def _ffn(x2d, g, hist, wup, fcw, fcb, wdn, gf, *, nseq, tm, rs, final_norm):
    m = x2d.shape[0]
    nt = m // (nseq * tm)
    hr = hist.shape[1]
    ck = 256
    row = lambda s, i: (s * nt + i, 0)
    rows = lambda c: pl.BlockSpec((tm, c), row)
    state_spec = pl.BlockSpec((None, hr, 2 * D_FF), lambda s, i: (s, 0, 0))
    return pl.pallas_call(
        functools.partial(_ffn_kernel, tm=tm, rs=rs, hr=hr, ck=ck, final_norm=final_norm),
        grid=(nseq, nt),
        in_specs=[rows(D_MODEL), _const_spec((1, D_MODEL)), state_spec,
                  _const_spec((D_MODEL, 2 * D_FF)), _const_spec((SUBLANES, 2 * D_FF)),
                  _const_spec((1, 2 * D_FF)), _const_spec((D_FF, D_MODEL)), _const_spec((1, D_MODEL))],
        out_specs=(rows(D_MODEL), state_spec),
        out_shape=(jax.ShapeDtypeStruct((m, D_MODEL), F32), jax.ShapeDtypeStruct((nseq, hr, 2 * D_FF), F32)),
        scratch_shapes=[pltpu.VMEM((tm, D_MODEL), BF16), pltpu.VMEM((hr, 2 * D_FF), F32),
                        pltpu.VMEM((tm, D_MODEL), F32)] + [pltpu.VMEM((hr + tm, ck), F32)] * 4,
        compiler_params=_params(2),
        name="ffn",
    )(x2d, g, hist, wup, fcw, fcb, wdn, gf)


def _round_up(n, k):
    return -(-n // k) * k


def _layer_weights(l, norm1_g, w_in, b_forget, sgu_ln_g, sgu_ln_b, sgu_w, sgu_b, w_proj_a, w_proj_b,
                   conv_w, conv_b, conv_ln_g, conv_ln_b, w_proj_c, w_out, norm2_g, w_up,
                   ffn_conv_w, ffn_conv_b, w_down):
    w = w_in[l]
    scale = HEAD_DIM ** -0.5 * LOG2E
    w_r = jnp.concatenate([
        w[:, :R_Q], w[:, R_Q:R_K] * scale, w[:, R_K:OFF_F], w[:, OFF_GLU:IN_COLS],
        w[:, OFF_F:OFF_GLU], jnp.zeros((D_MODEL, LANES - N_HEADS), F32)], axis=1).astype(BF16)
    bf = jnp.pad(b_forget[l], (0, LANES - N_HEADS))[None, :]
    causal = jnp.tril(jnp.ones((CHUNK, CHUNK), dtype=bool))
    return dict(
        g1=norm1_g[l][None, :], w_in=w_r, bf=bf, lng=sgu_ln_g[l][None, :], lnb=sgu_ln_b[l][None, :],
        sgu_w=jnp.where(causal, sgu_w[l], 0.0), sgu_b=sgu_b[l],
        wa=w_proj_a[l].astype(BF16), wb=w_proj_b[l].astype(BF16), wc=w_proj_c[l].astype(BF16),
        cw=jnp.pad(conv_w[l], ((0, 1), (0, 0))), cb=conv_b[l][None, :],
        cg=conv_ln_g[l][None, :], cbeta=conv_ln_b[l][None, :],
        wo=w_out[l].astype(BF16), g2=norm2_g[l][None, :], wup=w_up[l].astype(BF16),
        fcw=jnp.pad(ffn_conv_w[l], ((0, SUBLANES - FFN_CONV_W), (0, 0))), fcb=ffn_conv_b[l][None, :],
        wdn=w_down[l].astype(BF16))


def _mix_bias(sgu_b_rows):
    return jnp.repeat(sgu_b_rows.T, SGU_WIDTH // SGU_GROUPS, axis=1)


def kernel(x_prompt, x_sample, cache_k, cache_v, cache_logf, state_conv, state_ffn, page_table, norm1_g, w_in, b_forget, sgu_ln_g, sgu_ln_b, sgu_w, sgu_b, w_proj_a, w_proj_b, conv_w, conv_b, conv_ln_g, conv_ln_b, w_proj_c, w_out, norm2_g, w_up, ffn_conv_w, ffn_conv_b, w_down, norm_f_g):
    bp, seq, _ = x_prompt.shape
    bd, t_new, _ = x_sample.shape
    depth = w_in.shape[0]
    assert seq % CHUNK == 0 and PAGE_SIZE == CHUNK and t_new <= CHUNK
    assert seq >= CONV_W - 1 and t_new >= FFN_CONV_W - 1 and t_new <= CONV_W - 1

    tm_in, tm_mix, tm_ffn = 512, 512, 1024
    tq = 512
    attn_pairs = 2
    ms = bd * t_new
    assert ms == CHUNK and page_table.shape[1] % PAGES_PER_STEP == 0

    hr_conv_p = _round_up(CONV_W - 1, SUBLANES)
    hr_ffn_p = _round_up(FFN_CONV_W - 1, SUBLANES)
    hr_conv_s = (CONV_W - 1) * bd
    hr_ffn_s = (FFN_CONV_W - 1) * bd

    tri_p = jnp.tril(jnp.ones((tm_in, tm_in), F32)).astype(BF16)
    eye_b = jnp.eye(bd, dtype=F32)
    tri_s = jnp.kron(jnp.tril(jnp.ones((t_new, t_new), F32)), eye_b).astype(BF16)
    cache_kt = jnp.transpose(cache_k, (0, 1, 3, 4, 2))
    cache_vt = jnp.transpose(cache_v, (0, 1, 3, 4, 2))
    cache_lft = jnp.transpose(cache_logf, (0, 1, 3, 2))
    head_rows = (jnp.arange(t_new * N_HEADS)[:, None] % N_HEADS) == (jnp.arange(ATT_WIDTH)[None, :] // HEAD_DIM)
    n_new = _round_up(t_new, SUBLANES)
    gf = norm_f_g[None, :]

    xp = x_prompt.reshape(bp * seq, D_MODEL)
    xs = jnp.transpose(x_sample, (1, 0, 2)).reshape(ms, D_MODEL)

    def to_seq_major(a):
        return jnp.transpose(a.reshape(t_new, bd, a.shape[-1]), (1, 0, 2))

    outs = {k: [] for k in ("fp", "cp", "ffp", "fs", "avs", "cs", "ffs")}
    kt_p = jnp.zeros((bp, depth, ATT_WIDTH, seq), F32)
    vt_p = jnp.zeros((bp, depth, ATT_WIDTH, seq), F32)
    kt_s = jnp.zeros((1, depth, ATT_WIDTH, ms), F32)
    vt_s = jnp.zeros((1, depth, ATT_WIDTH, ms), F32)
    for l in range(depth):
        lw = _layer_weights(l, norm1_g, w_in, b_forget, sgu_ln_g, sgu_ln_b, sgu_w, sgu_b, w_proj_a, w_proj_b,
                            conv_w, conv_b, conv_ln_g, conv_ln_b, w_proj_c, w_out, norm2_g, w_up,
                            ffn_conv_w, ffn_conv_b, w_down)
        last = l == depth - 1

        au, av, qx, kx, lf, c, glu, gate, kt_p, vt_p = _inproj(
            xp, lw["g1"], lw["w_in"], lw["bf"], lw["lng"], lw["lnb"], tri_p, kt_p, vt_p, l, nseq=bp, tm=tm_in)
        attn = _attn_prompt(qx.reshape(bp, seq, QKX_WIDTH), kx.reshape(bp, seq, QKX_WIDTH), vt_p, l,
                            tq=tq, pairs=attn_pairs)
        xp, cstate = _mix(au, av, attn.reshape(bp * seq, ATT_WIDTH), glu, gate, xp,
                          jnp.zeros((bp, hr_conv_p, CONV_CH), F32),
                          lw["sgu_w"].astype(BF16), _mix_bias(lw["sgu_b"]),
                          lw["wa"], lw["wb"], lw["cw"], lw["cb"], lw["cg"], lw["cbeta"], lw["wc"], lw["wo"],
                          nseq=bp, tm=tm_mix, rs=1)
        xp, fstate = _ffn(xp, lw["g2"], jnp.zeros((bp, hr_ffn_p, 2 * D_FF), F32), lw["wup"], lw["fcw"],
                          lw["fcb"], lw["wdn"], gf, nseq=bp, tm=tm_ffn, rs=1, final_norm=last)
        outs["fp"].append(lf.reshape(bp, seq, N_HEADS))
        outs["cp"].append(cstate[:, hr_conv_p - (CONV_W - 1):])
        outs["ffp"].append(fstate[:, hr_ffn_p - (FFN_CONV_W - 1):])

        au, av, qx, kx, lf, c, glu, gate, kt_s, vt_s = _inproj(
            xs, lw["g1"], lw["w_in"], lw["bf"], lw["lng"], lw["lnb"], tri_s, kt_s, vt_s, l, nseq=1, tm=ms)
        k, v = kt_s[0, l].T, vt_s[0, l].T
        q = qx.reshape(ms, N_HEADS // 2, PAIR_X)[:, :, :LANES].reshape(ms, ATT_WIDTH)
        q_bd = jnp.where(head_rows, jnp.repeat(to_seq_major(q), N_HEADS, axis=1), jnp.zeros((), BF16))
        pad_new = lambda a: jnp.pad(to_seq_major(a).astype(BF16), ((0, 0), (0, n_new - t_new), (0, 0)))
        cn = jnp.tile(jnp.transpose(to_seq_major(c), (0, 2, 1)), (1, t_new, 1))
        cn = jnp.pad(cn, ((0, 0), (0, 0), (0, n_new - t_new)))
        attn = _attn_sample(page_table, l, q_bd, pad_new(k), pad_new(v), cn,
                            cache_kt, cache_vt, cache_lft, npp=PAGES_PER_STEP)
        attn = jnp.transpose(attn, (1, 0, 2)).reshape(ms, ATT_WIDTH).astype(BF16)
        mixw_s = jnp.stack([jnp.kron(lw["sgu_w"][g, :t_new, :t_new], eye_b) for g in range(SGU_GROUPS)])
        mixb_s = _mix_bias(jnp.repeat(lw["sgu_b"][:, :t_new], bd, axis=1))
        hist_c = jnp.transpose(state_conv[:, l], (1, 0, 2)).reshape(1, hr_conv_s, CONV_CH)
        xs, cstate = _mix(au, av, attn, glu, gate, xs, hist_c, mixw_s.astype(BF16), mixb_s,
                          lw["wa"], lw["wb"], lw["cw"], lw["cb"], lw["cg"], lw["cbeta"], lw["wc"], lw["wo"],
                          nseq=1, tm=ms, rs=bd)
        hist_f = jnp.transpose(state_ffn[:, l], (1, 0, 2)).reshape(1, hr_ffn_s, 2 * D_FF)
        xs, fstate = _ffn(xs, lw["g2"], hist_f, lw["wup"], lw["fcw"], lw["fcb"], lw["wdn"], gf,
                          nseq=1, tm=ms, rs=bd, final_norm=last)
        outs["fs"].append(to_seq_major(lf))
        outs["avs"].append(to_seq_major(av))
        outs["cs"].append(jnp.transpose(cstate.reshape(CONV_W - 1, bd, CONV_CH), (1, 0, 2)))
        outs["ffs"].append(jnp.transpose(fstate.reshape(FFN_CONV_W - 1, bd, 2 * D_FF), (1, 0, 2)))

    y_prompt = xp.reshape(bp, seq, D_MODEL)
    y_sample = to_seq_major(xs)
    st = lambda key: jnp.stack(outs[key], axis=1)

    def heads_last_p(a):
        return jnp.transpose(a.reshape(bp, depth, N_HEADS, HEAD_DIM, seq), (0, 1, 4, 2, 3))

    def heads_last_s(a):
        return jnp.transpose(a.reshape(depth, N_HEADS, HEAD_DIM, t_new, bd), (4, 0, 3, 1, 2))

    return (y_prompt, y_sample, heads_last_p(kt_p), heads_last_p(vt_p), st("fp"), st("cp"), st("ffp"),
            heads_last_s(kt_s), heads_last_s(vt_s), st("fs"), st("avs"), st("cs"), st("ffs"))
```

```python
import functools

import jax
import jax.numpy as jnp
from jax import lax
from jax.experimental import pallas as pl
from jax.experimental.pallas import tpu as pltpu

F32 = jnp.float32
BF16 = jnp.bfloat16

D_MODEL = 1024
SGU_WIDTH = 256
SGU_GROUPS = 4
CHUNK = 128
N_HEADS = 8
HEAD_DIM = 64
ATT_WIDTH = N_HEADS * HEAD_DIM
CONV_CH = 256
CONV_W = 31
D_FF = 2816
FFN_CONV_W = 3
N_BRANCH = 3
PAGE_SIZE = 128
EPS = 1e-6
NEG_INF = -1e30
LOG2E = 1.4426950408889634

OFF_F = 2 * SGU_WIDTH + 3 * ATT_WIDTH
OFF_GLU = OFF_F + N_HEADS
IN_COLS = OFF_GLU + 2 * CONV_CH + N_BRANCH * D_MODEL

LANES = 128
SUBLANES = 8
R_Q = 2 * SGU_WIDTH
R_K = R_Q + ATT_WIDTH
R_V = R_K + ATT_WIDTH
R_GLU = R_V + ATT_WIDTH
R_GATE = R_GLU + 2 * CONV_CH
R_F = R_GATE + N_BRANCH * D_MODEL
R_COLS = R_F + LANES

PAIR_X = 2 * LANES
QKX_WIDTH = (N_HEADS // 2) * PAIR_X
AUG_LANES = 6 * N_HEADS

VMEM_LIMIT_BYTES = 56 * 1024 * 1024
PAGES_PER_STEP = 32
FFN_AHEAD_SLOTS = 3


def _dot(a, b):
    return jnp.dot(a, b, preferred_element_type=F32)


def _dot_nt(a, b):
    return lax.dot_general(a, b, (((1,), (1,)), ((), ())), preferred_element_type=F32)


def _split3(x):
    hi = x.astype(BF16)
    r1 = x - hi.astype(F32)
    mid = r1.astype(BF16)
    lo = (r1 - mid.astype(F32)).astype(BF16)
    return hi, mid, lo


def _dot01(m01, x):
    hi, mid, lo = _split3(x)
    return _dot(m01, hi) + _dot(m01, mid) + _dot(m01, lo)


def _dot01_right(x, m01):
    hi, mid, lo = _split3(x)
    return _dot(hi, m01) + _dot(mid, m01) + _dot(lo, m01)


def _rms(x, g):
    return x * lax.rsqrt(jnp.mean(x * x, axis=-1, keepdims=True) + EPS) * g


def _layer_norm(x, g, b):
    mu = jnp.mean(x, axis=-1, keepdims=True)
    xc = x - mu
    var = jnp.mean(xc * xc, axis=-1, keepdims=True)
    return xc * lax.rsqrt(var + EPS) * g + b


def _log_sigmoid(x):
    return jnp.minimum(x, 0.0) - jnp.log1p(jnp.exp(-jnp.abs(x)))


def _const_spec(shape):
    nd = len(shape)
    return pl.BlockSpec(shape, lambda *_: (0,) * nd, pipeline_mode=pl.Buffered(1))


def _params(n_axes):
    return pltpu.CompilerParams(dimension_semantics=("arbitrary",) * n_axes,
                                vmem_limit_bytes=VMEM_LIMIT_BYTES)


def _inproj_kernel(x_ref, g_ref, w_ref, bf_ref, lng_ref, lnb_ref, tri_ref, kt_in_ref, vt_in_ref,
                   au_ref, av_ref, qx_ref, kx_ref, lf_ref, c_ref, glu_ref, gate_ref, kt_ref, vt_ref,
                   h_ref, carry_ref, *, tm):
    del kt_in_ref, vt_in_ref
    i = pl.program_id(1)
    h_ref[...] = _rms(x_ref[...], g_ref[...]).astype(BF16)

    def seg(c0, n):
        return _dot(h_ref[...], w_ref[:, c0:c0 + n])

    au_ref[...] = jax.nn.gelu(seg(0, SGU_WIDTH))
    av_ref[...] = _layer_norm(jax.nn.gelu(seg(SGU_WIDTH, SGU_WIDTH)), lng_ref[...], lnb_ref[...])
    qq = seg(R_Q, ATT_WIDTH).astype(BF16)
    kk = seg(R_K, ATT_WIDTH)
    kt_ref[...] = kk.T
    kkb = kk.astype(BF16)
    for p in range(N_HEADS // 2):
        qx_ref[:, p * PAIR_X:p * PAIR_X + LANES] = qq[:, p * LANES:(p + 1) * LANES]
        kx_ref[:, p * PAIR_X:p * PAIR_X + LANES] = kkb[:, p * LANES:(p + 1) * LANES]
    vt_ref[...] = seg(R_V, ATT_WIDTH).T
    zg = seg(R_GLU, 2 * CONV_CH)
    glu_ref[...] = zg[:, :CONV_CH] * jax.nn.sigmoid(zg[:, CONV_CH:])
    gate_chunk = 512
    for c in range(N_BRANCH * D_MODEL // gate_chunk):
        gate_ref[:, c * gate_chunk:(c + 1) * gate_chunk] = jax.nn.sigmoid(seg(R_GATE + c * gate_chunk, gate_chunk))
    lf = _log_sigmoid(seg(R_F, LANES) + bf_ref[...])
    lf_ref[...] = lf[:, :N_HEADS]

    @pl.when(i == 0)
    def _():
        carry_ref[...] = jnp.zeros_like(carry_ref)

    c = _dot01(tri_ref[...], lf) + carry_ref[...]
    c_ref[...] = c[:, :N_HEADS]
    carry_ref[...] = c[tm - 1:tm, :]

    lane = lax.broadcasted_iota(jnp.int32, (1, LANES), 1)
    hi, mid, lo = (part.astype(F32) for part in _split3(jnp.where(lane < N_HEADS, c * LOG2E, 0.0)))
    kaug = (jnp.where((lane >= 3 * N_HEADS) & (lane < AUG_LANES), 1.0, 0.0)
            - (hi + pltpu.roll(mid, N_HEADS, 1) + pltpu.roll(lo, 2 * N_HEADS, 1))).astype(BF16)
    qaug = (jnp.where(lane < 3 * N_HEADS, 1.0, 0.0) + pltpu.roll(hi, 3 * N_HEADS, 1)
            + pltpu.roll(mid, 4 * N_HEADS, 1) + pltpu.roll(lo, 5 * N_HEADS, 1)).astype(BF16)
    for p in range(N_HEADS // 2):
        qx_ref[:, p * PAIR_X + LANES:(p + 1) * PAIR_X] = qaug
        kx_ref[:, p * PAIR_X + LANES:(p + 1) * PAIR_X] = kaug


def _inproj(x2d, g, w, bf, lng, lnb, tri, kt_all, vt_all, layer, *, nseq, tm):
    m = x2d.shape[0]
    nt = m // (nseq * tm)
    row = lambda s, i: (s * nt + i, 0)

    def rows(c):
        return pl.BlockSpec((tm, c), row)

    out_shape = (
        jax.ShapeDtypeStruct((m, SGU_WIDTH), F32), jax.ShapeDtypeStruct((m, SGU_WIDTH), F32),
        jax.ShapeDtypeStruct((m, QKX_WIDTH), BF16), jax.ShapeDtypeStruct((m, QKX_WIDTH), BF16),
        jax.ShapeDtypeStruct((m, N_HEADS), F32), jax.ShapeDtypeStruct((m, N_HEADS), F32),
        jax.ShapeDtypeStruct((m, CONV_CH), F32), jax.ShapeDtypeStruct((m, N_BRANCH * D_MODEL), F32),
        jax.ShapeDtypeStruct(kt_all.shape, F32), jax.ShapeDtypeStruct(vt_all.shape, F32),
    )
    slab = pl.BlockSpec((None, None, ATT_WIDTH, tm), lambda s, i: (s, layer, 0, i))
    out_specs = (rows(SGU_WIDTH), rows(SGU_WIDTH), rows(QKX_WIDTH), rows(QKX_WIDTH), rows(N_HEADS),
                 rows(N_HEADS), rows(CONV_CH), rows(N_BRANCH * D_MODEL), slab, slab)
    any_spec = pl.BlockSpec(memory_space=pl.ANY)
    return pl.pallas_call(
        functools.partial(_inproj_kernel, tm=tm),
        grid=(nseq, nt),
        in_specs=[rows(D_MODEL), _const_spec((1, D_MODEL)), _const_spec((D_MODEL, R_COLS)),
                  _const_spec((1, LANES)), _const_spec((1, SGU_WIDTH)), _const_spec((1, SGU_WIDTH)),
                  _const_spec((tm, tm)), any_spec, any_spec],
        out_specs=out_specs,
        out_shape=out_shape,
        input_output_aliases={7: 8, 8: 9},
        scratch_shapes=[pltpu.VMEM((tm, D_MODEL), BF16), pltpu.VMEM((1, LANES), F32)],
        compiler_params=_params(2),
        name="inproj",
    )(x2d, g, w, bf, lng, lnb, tri, kt_all, vt_all)


def _attn_prompt_kernel(qx_ref, kx_ref, vt_ref, o_ref, qm_ref, m_ref, l_ref, acc_ref, s_ref, *, tq, pairs):
    grp = pl.program_id(1)
    i = pl.program_id(2)
    nh = 2 * pairs
    lane_x = lax.broadcasted_iota(jnp.int32, (1, PAIR_X), 1)
    key_ids = lax.broadcasted_iota(jnp.int32, (tq, tq), 0)
    qry_ids = lax.broadcasted_iota(jnp.int32, (tq, tq), 1)
    for hl in range(nh):
        pr, hh = divmod(hl, 2)
        head = grp * nh + hl
        own = ((lane_x // HEAD_DIM) == hh) | (
            (lane_x >= LANES) & (lane_x < LANES + AUG_LANES) & ((lane_x - LANES) % N_HEADS == head))
        qm_ref[hl] = qx_ref[:, pr * PAIR_X:(pr + 1) * PAIR_X] * jnp.where(own, 1.0, 0.0).astype(BF16)
    m_ref[...] = jnp.full_like(m_ref, NEG_INF)
    l_ref[...] = jnp.zeros_like(l_ref)
    acc_ref[...] = jnp.zeros_like(acc_ref)

    def scores(start, hl):
        pr = hl // 2
        return _dot_nt(kx_ref[pl.ds(start, tq), pr * PAIR_X:(pr + 1) * PAIR_X], qm_ref[hl])

    s_ref[0] = scores(0, 0)

    def step(j, diagonal):
        start = pl.multiple_of(j * tq, tq)
        for hl in range(nh):
            if hl + 1 < nh:
                s_ref[(hl + 1) % 2] = scores(start, hl + 1)
            elif not diagonal:
                s_ref[0] = scores(pl.multiple_of(start + tq, tq), 0)
            st = s_ref[hl % 2]
            if diagonal:
                st = jnp.where(key_ids <= qry_ids, st, NEG_INF)
            m_old = m_ref[hl]
            m_new = jnp.maximum(m_old, jnp.max(st, axis=0, keepdims=True))
            alpha = jnp.exp2(m_old - m_new)
            p = jnp.exp2(st - m_new)
            l_ref[hl] = alpha * l_ref[hl] + jnp.sum(p, axis=0, keepdims=True)
            vt = vt_ref[hl * HEAD_DIM:(hl + 1) * HEAD_DIM, pl.ds(start, tq)].astype(BF16)
            acc_ref[hl] = alpha * acc_ref[hl] + _dot(vt, p.astype(BF16))
            m_ref[hl] = m_new

    def body(j, carry):
        step(j, False)
        return carry

    lax.fori_loop(0, i, body, 0)
    step(i, True)
    for pr in range(pairs):
        out_t = jnp.concatenate([acc_ref[2 * pr] / l_ref[2 * pr], acc_ref[2 * pr + 1] / l_ref[2 * pr + 1]], axis=0)
        o_ref[:, pr * LANES:(pr + 1) * LANES] = out_t.T.astype(o_ref.dtype)


def _attn_prompt(qx, kx, vt_all, layer, *, tq, pairs):
    b, s, _ = qx.shape
    ngrp = N_HEADS // (2 * pairs)
    nh = 2 * pairs
    return pl.pallas_call(
        functools.partial(_attn_prompt_kernel, tq=tq, pairs=pairs),
        grid=(b, ngrp, s // tq),
        in_specs=[
            pl.BlockSpec((None, tq, pairs * PAIR_X), lambda bb, g, i: (bb, i, g)),
            pl.BlockSpec((None, s, pairs * PAIR_X), lambda bb, g, i: (bb, 0, g)),
            pl.BlockSpec((None, None, pairs * LANES, s), lambda bb, g, i: (bb, layer, g, 0)),
        ],
        out_specs=pl.BlockSpec((None, tq, pairs * LANES), lambda bb, g, i: (bb, i, g)),
        out_shape=jax.ShapeDtypeStruct((b, s, ATT_WIDTH), BF16),
        scratch_shapes=[pltpu.VMEM((nh, tq, PAIR_X), BF16), pltpu.VMEM((nh, 1, tq), F32),
                        pltpu.VMEM((nh, 1, tq), F32), pltpu.VMEM((nh, HEAD_DIM, tq), F32),
                        pltpu.VMEM((2, tq, tq), F32)],
        compiler_params=_params(3),
        name="attn_prompt",
    )(qx, kx, vt_all)


def _attn_sample_kernel(pt_ref, q_ref, kn_ref, vn_ref, cn_ref, *rest, npp, t_new):
    del pt_ref
    k_refs = rest[:npp]
    v_refs = rest[npp:2 * npp]
    lf_refs = rest[2 * npp:3 * npp]
    o_ref = rest[3 * npp]
    m_ref, l_ref, acc_ref, run_ref = rest[3 * npp + 1:]
    j = pl.program_id(1)
    nrow = t_new * N_HEADS

    @pl.when(j == 0)
    def _():
        m_ref[...] = jnp.full_like(m_ref, NEG_INF)
        l_ref[...] = jnp.zeros_like(l_ref)
        acc_ref[...] = jnp.zeros_like(acc_ref)
        run_ref[...] = jnp.zeros_like(run_ref)

    q = q_ref[...]

    def online_update(s, pv):
        m_old = m_ref[...]
        m_new = jnp.maximum(m_old, jnp.max(s, axis=-1, keepdims=True))
        alpha = jnp.exp2(m_old - m_new)
        p = jnp.exp2(s - m_new)
        l_ref[...] = alpha * l_ref[...] + jnp.sum(p, axis=-1, keepdims=True)
        acc_ref[...] = alpha * acc_ref[...] + pv(p.astype(BF16))
        m_ref[...] = m_new

    lf_all = jnp.concatenate([lf_refs[pi][...] for pi in range(npp)], axis=0) * LOG2E
    later = (lax.broadcasted_iota(jnp.int32, (PAGE_SIZE, PAGE_SIZE), 0)
             > lax.broadcasted_iota(jnp.int32, (PAGE_SIZE, PAGE_SIZE), 1))
    suf_all = _dot01_right(lf_all, jnp.where(later, 1.0, 0.0).astype(BF16))
    tot_all = suf_all[:, 0:1] + lf_all[:, 0:1]
    run = run_ref[...]
    pieces = []
    for pi in range(npp):
        r = suf_all[pi * N_HEADS:(pi + 1) * N_HEADS] + run
        run = run + tot_all[pi * N_HEADS:(pi + 1) * N_HEADS]
        kp = k_refs[pi][...].reshape(ATT_WIDTH, PAGE_SIZE).astype(BF16)
        pieces.append(_dot(q, kp) + jnp.concatenate([r] * t_new, axis=0))
    run_ref[...] = run

    def pv_pages(p):
        out = None
        for pi in range(npp):
            vp = v_refs[pi][...].reshape(ATT_WIDTH, PAGE_SIZE).astype(BF16)
            term = _dot_nt(p[:, pi * PAGE_SIZE:(pi + 1) * PAGE_SIZE], vp)
            out = term if out is None else out + term
        return out

    online_update(jnp.concatenate(pieces, axis=-1), pv_pages)

    @pl.when(j == pl.num_programs(1) - 1)
    def _():
        ln = _dot_nt(q, kn_ref[...]) - cn_ref[...] * LOG2E
        rr = lax.broadcasted_iota(jnp.int32, ln.shape, 0)
        cc = lax.broadcasted_iota(jnp.int32, ln.shape, 1)
        online_update(jnp.where(cc <= rr // N_HEADS, ln, NEG_INF), lambda p: _dot(p, vn_ref[...]))
        full = acc_ref[...] / l_ref[...]
        row_head = lax.broadcasted_iota(jnp.int32, full.shape, 0) % N_HEADS
        lane_head = lax.broadcasted_iota(jnp.int32, full.shape, 1) // HEAD_DIM
        full = jnp.where(row_head == lane_head, full, 0.0)
        for t in range(t_new):
            o_ref[t:t + 1, :] = jnp.sum(full[t * N_HEADS:(t + 1) * N_HEADS], axis=0, keepdims=True)


def _attn_sample(page_table, layer, q, kn, vn, cn, cache_kt, cache_vt, cache_lft, *, npp):
    bd, nrow, _ = q.shape
    n_pages = page_table.shape[1]
    nsteps = n_pages // npp
    t_new = nrow // N_HEADS
    n_new = kn.shape[1]

    def page_idx(b, j, pt, pi):
        return pt[b, n_pages - 1 - (j * npp + pi)]

    def kv_spec(pi):
        return pl.BlockSpec((None, None, N_HEADS, HEAD_DIM, PAGE_SIZE),
                            lambda b, j, pt: (page_idx(b, j, pt, pi), layer, 0, 0, 0))

    def lf_spec(pi):
        return pl.BlockSpec((None, None, N_HEADS, PAGE_SIZE),
                            lambda b, j, pt: (page_idx(b, j, pt, pi), layer, 0, 0))

    def seq_spec(r, c):
        return pl.BlockSpec((None, r, c), lambda b, j, pt: (b, 0, 0))

    grid_spec = pltpu.PrefetchScalarGridSpec(
        num_scalar_prefetch=1,
        grid=(bd, nsteps),
        in_specs=[seq_spec(nrow, ATT_WIDTH), seq_spec(n_new, ATT_WIDTH), seq_spec(n_new, ATT_WIDTH),
                  seq_spec(nrow, n_new)]
        + [kv_spec(pi) for pi in range(npp)] + [kv_spec(pi) for pi in range(npp)]
        + [lf_spec(pi) for pi in range(npp)],
        out_specs=seq_spec(t_new, ATT_WIDTH),
        scratch_shapes=[pltpu.VMEM((nrow, 1), F32), pltpu.VMEM((nrow, 1), F32),
                        pltpu.VMEM((nrow, ATT_WIDTH), F32), pltpu.VMEM((N_HEADS, 1), F32)],
    )
    return pl.pallas_call(
        functools.partial(_attn_sample_kernel, npp=npp, t_new=t_new),
        grid_spec=grid_spec,
        out_shape=jax.ShapeDtypeStruct((bd, t_new, ATT_WIDTH), F32),
        compiler_params=_params(2),
        name="attn_sample",
    )(page_table, q, kn, vn, cn, *([cache_kt] * npp), *([cache_vt] * npp), *([cache_lft] * npp))


def _mix_kernel(au_ref, av_ref, at_ref, glu_ref, gate_ref, x_ref, hist_ref,
                mixw_ref, mixb_ref, wa_ref, wb_ref, cw_ref, cb_ref, cg_ref, cbeta_ref, wc_ref, wo_ref,
                xo_ref, cstate_ref, buf_ref, ua_ref, cs_ref, *, tm, rs, hr):
    i = pl.program_id(1)

    @pl.when(i == 0)
    def _():
        buf_ref[0:hr, :] = hist_ref[...]

    buf_ref[hr:hr + tm, :] = glu_ref[...]

    off = hr - (CONV_W - 1) * rs
    rc = 64
    for r0 in range(0, tm, rc):
        acc = jnp.broadcast_to(cb_ref[...], (rc, CONV_CH))
        for j in range(CONV_W):
            acc = acc + cw_ref[j:j + 1, :] * buf_ref[off + j * rs + r0:off + j * rs + r0 + rc, :]
        y = _layer_norm(acc, cg_ref[...], cbeta_ref[...])
        cs_ref[r0:r0 + rc, :] = (y * jax.nn.sigmoid(y)).astype(BF16)
    cstate_ref[...] = buf_ref[tm:tm + hr, :]
    buf_ref[0:hr, :] = buf_ref[tm:tm + hr, :]

    lane_group = lax.broadcasted_iota(jnp.int32, (CHUNK, SGU_WIDTH), 1) // (SGU_WIDTH // SGU_GROUPS)
    for c0 in range(0, tm, CHUNK):
        vc = av_ref[c0:c0 + CHUNK, :].astype(BF16)
        mixed = _dot(mixw_ref[0], vc)
        for g in range(1, SGU_GROUPS):
            mixed = jnp.where(lane_group == g, _dot(mixw_ref[g], vc), mixed)
        ua_ref[c0:c0 + CHUNK, :] = (au_ref[c0:c0 + CHUNK, :] * (mixed + mixb_ref[...])).astype(BF16)

    ya = _dot(ua_ref[...], wa_ref[...])
    yb = _dot(at_ref[...], wb_ref[...])
    yc = _dot(cs_ref[...], wc_ref[...])
    merged = (gate_ref[:, 0:D_MODEL] * ya + gate_ref[:, D_MODEL:2 * D_MODEL] * yb
              + gate_ref[:, 2 * D_MODEL:3 * D_MODEL] * yc)
    xo_ref[...] = x_ref[...] + _dot(merged.astype(BF16), wo_ref[...])


def _mix(au, av, attn, glu, gate, x2d, hist, mixw, mixb, wa, wb, cw, cb, cg, cbeta, wc, wo, *, nseq, tm, rs):
    m = x2d.shape[0]
    nt = m // (nseq * tm)
    hr = hist.shape[1]
    row = lambda s, i: (s * nt + i, 0)
    rows = lambda c: pl.BlockSpec((tm, c), row)
    return pl.pallas_call(
        functools.partial(_mix_kernel, tm=tm, rs=rs, hr=hr),
        grid=(nseq, nt),
        in_specs=[rows(SGU_WIDTH), rows(SGU_WIDTH), rows(ATT_WIDTH), rows(CONV_CH), rows(N_BRANCH * D_MODEL),
                  rows(D_MODEL), pl.BlockSpec((None, hr, CONV_CH), lambda s, i: (s, 0, 0)),
                  _const_spec((SGU_GROUPS, CHUNK, CHUNK)), _const_spec((CHUNK, SGU_WIDTH)),
                  _const_spec((SGU_WIDTH, D_MODEL)), _const_spec((ATT_WIDTH, D_MODEL)),
                  _const_spec((CONV_W + 1, CONV_CH)), _const_spec((1, CONV_CH)), _const_spec((1, CONV_CH)),
                  _const_spec((1, CONV_CH)), _const_spec((CONV_CH, D_MODEL)), _const_spec((D_MODEL, D_MODEL))],
        out_specs=(rows(D_MODEL), pl.BlockSpec((None, hr, CONV_CH), lambda s, i: (s, 0, 0))),
        out_shape=(jax.ShapeDtypeStruct((m, D_MODEL), F32), jax.ShapeDtypeStruct((nseq, hr, CONV_CH), F32)),
        scratch_shapes=[pltpu.VMEM((hr + tm, CONV_CH), F32), pltpu.VMEM((tm, SGU_WIDTH), BF16),
                        pltpu.VMEM((tm, CONV_CH), BF16)],
        compiler_params=_params(2),
        name="mix",
    )(au, av, attn, glu, gate, x2d, hist, mixw, mixb, wa, wb, cw, cb, cg, cbeta, wc, wo)


def _ffn_kernel(x_ref, g_ref, hist_ref, wup_ref, fcw_ref, fcb_ref, wdn_ref, gf_ref,
                xo_ref, fstate_ref, h_ref, carry_ref, acc_ref, *scratch_refs, tm, rs, hr, ck, final_norm):
    act_refs, buf_refs = scratch_refs[:2], scratch_refs[2:]
    i = pl.program_id(1)

    @pl.when(i == 0)
    def _():
        carry_ref[...] = hist_ref[...]

    x = x_ref[...]
    h_ref[...] = _rms(x, g_ref[...]).astype(BF16)
    acc_ref[...] = x
    nck = D_FF // ck

    def up_project(c):
        for half in range(2):
            col = half * D_FF + c * ck
            slot = 2 * (c % FFN_AHEAD_SLOTS) + half
            buf_refs[slot][0:hr, :] = carry_ref[:, col:col + ck]
            buf_refs[slot][hr:hr + tm, :] = _dot(h_ref[...], wup_ref[:, col:col + ck])
            carry_ref[:, col:col + ck] = buf_refs[slot][tm:tm + hr, :]

    for c in range(FFN_AHEAD_SLOTS - 1):
        up_project(c)
    def down_project(c):
        acc_ref[...] += _dot(act_refs[c % 2][...], wdn_ref[c * ck:(c + 1) * ck, :])

    for c in range(nck):
        if c + FFN_AHEAD_SLOTS - 1 < nck:
            up_project(c + FFN_AHEAD_SLOTS - 1)
        if c >= 1:
            down_project(c - 1)
        halves = []
        for half in range(2):
            col = half * D_FF + c * ck
            slot = 2 * (c % FFN_AHEAD_SLOTS) + half
            conv = fcb_ref[:, col:col + ck]
            for j in range(FFN_CONV_W):
                o = hr - (FFN_CONV_W - 1 - j) * rs
                conv = conv + fcw_ref[j:j + 1, col:col + ck] * buf_refs[slot][o:o + tm, :]
            halves.append(conv)
        act_refs[c % 2][...] = (jax.nn.gelu(halves[0]) * halves[1]).astype(BF16)
    down_project(nck - 1)
    fstate_ref[...] = carry_ref[...]
    out = acc_ref[...]
    if final_norm:
        out = _rms(out, gf_ref[...])
    xo_ref[...] = out


def _ffn(x2d, g, hist, wup, fcw, fcb, wdn, gf, *, nseq, tm, rs, final_norm):
    m = x2d.shape[0]
    nt = m // (nseq * tm)
    hr = hist.shape[1]
    ck = 256
    row = lambda s, i: (s * nt + i, 0)
    rows = lambda c: pl.BlockSpec((tm, c), row)
    state_spec = pl.BlockSpec((None, hr, 2 * D_FF), lambda s, i: (s, 0, 0))
    return pl.pallas_call(
        functools.partial(_ffn_kernel, tm=tm, rs=rs, hr=hr, ck=ck, final_norm=final_norm),
        grid=(nseq, nt),
        in_specs=[rows(D_MODEL), _const_spec((1, D_MODEL)), state_spec,
                  _const_spec((D_MODEL, 2 * D_FF)), _const_spec((SUBLANES, 2 * D_FF)),
                  _const_spec((1, 2 * D_FF)), _const_spec((D_FF, D_MODEL)), _const_spec((1, D_MODEL))],
        out_specs=(rows(D_MODEL), state_spec),
        out_shape=(jax.ShapeDtypeStruct((m, D_MODEL), F32), jax.ShapeDtypeStruct((nseq, hr, 2 * D_FF), F32)),
        scratch_shapes=[pltpu.VMEM((tm, D_MODEL), BF16), pltpu.VMEM((hr, 2 * D_FF), F32),
                        pltpu.VMEM((tm, D_MODEL), F32)] + [pltpu.VMEM((tm, ck), BF16)] * 2
        + [pltpu.VMEM((hr + tm, ck), F32)] * (2 * FFN_AHEAD_SLOTS),
        compiler_params=_params(2),
        name="ffn",
    )(x2d, g, hist, wup, fcw, fcb, wdn, gf)


def _round_up(n, k):
    return -(-n // k) * k


def _layer_weights(l, norm1_g, w_in, b_forget, sgu_ln_g, sgu_ln_b, sgu_w, sgu_b, w_proj_a, w_proj_b,
                   conv_w, conv_b, conv_ln_g, conv_ln_b, w_proj_c, w_out, norm2_g, w_up,
                   ffn_conv_w, ffn_conv_b, w_down):
    w = w_in[l]
    scale = HEAD_DIM ** -0.5 * LOG2E
    w_r = jnp.concatenate([
        w[:, :R_Q], w[:, R_Q:R_K] * scale, w[:, R_K:OFF_F], w[:, OFF_GLU:IN_COLS],
        w[:, OFF_F:OFF_GLU], jnp.zeros((D_MODEL, LANES - N_HEADS), F32)], axis=1).astype(BF16)
    bf = jnp.pad(b_forget[l], (0, LANES - N_HEADS))[None, :]
    causal = jnp.tril(jnp.ones((CHUNK, CHUNK), dtype=bool))
    return dict(
        g1=norm1_g[l][None, :], w_in=w_r, bf=bf, lng=sgu_ln_g[l][None, :], lnb=sgu_ln_b[l][None, :],
        sgu_w=jnp.where(causal, sgu_w[l], 0.0), sgu_b=sgu_b[l],
        wa=w_proj_a[l].astype(BF16), wb=w_proj_b[l].astype(BF16), wc=w_proj_c[l].astype(BF16),
        cw=jnp.pad(conv_w[l], ((0, 1), (0, 0))), cb=conv_b[l][None, :],
        cg=conv_ln_g[l][None, :], cbeta=conv_ln_b[l][None, :],
        wo=w_out[l].astype(BF16), g2=norm2_g[l][None, :], wup=w_up[l].astype(BF16),
        fcw=jnp.pad(ffn_conv_w[l], ((0, SUBLANES - FFN_CONV_W), (0, 0))), fcb=ffn_conv_b[l][None, :],
        wdn=w_down[l].astype(BF16))


def _mix_bias(sgu_b_rows):
    return jnp.repeat(sgu_b_rows.T, SGU_WIDTH // SGU_GROUPS, axis=1)


def kernel(x_prompt, x_sample, cache_k, cache_v, cache_logf, state_conv, state_ffn, page_table, norm1_g, w_in, b_forget, sgu_ln_g, sgu_ln_b, sgu_w, sgu_b, w_proj_a, w_proj_b, conv_w, conv_b, conv_ln_g, conv_ln_b, w_proj_c, w_out, norm2_g, w_up, ffn_conv_w, ffn_conv_b, w_down, norm_f_g):
    bp, seq, _ = x_prompt.shape
    bd, t_new, _ = x_sample.shape
    depth = w_in.shape[0]
    assert seq % CHUNK == 0 and PAGE_SIZE == CHUNK and t_new <= CHUNK
    assert seq >= CONV_W - 1 and t_new >= FFN_CONV_W - 1 and t_new <= CONV_W - 1

    tm_in, tm_mix, tm_ffn = 512, 512, 1024
    tq = 512
    attn_pairs = 2
    ms = bd * t_new
    assert ms == CHUNK and page_table.shape[1] % PAGES_PER_STEP == 0

    hr_conv_p = _round_up(CONV_W - 1, SUBLANES)
    hr_ffn_p = _round_up(FFN_CONV_W - 1, SUBLANES)
    hr_conv_s = (CONV_W - 1) * bd
    hr_ffn_s = (FFN_CONV_W - 1) * bd

    tri_p = jnp.tril(jnp.ones((tm_in, tm_in), F32)).astype(BF16)
    eye_b = jnp.eye(bd, dtype=F32)
    tri_s = jnp.kron(jnp.tril(jnp.ones((t_new, t_new), F32)), eye_b).astype(BF16)
    cache_kt = jnp.transpose(cache_k, (0, 1, 3, 4, 2))
    cache_vt = jnp.transpose(cache_v, (0, 1, 3, 4, 2))
    cache_lft = jnp.transpose(cache_logf, (0, 1, 3, 2))
    head_rows = (jnp.arange(t_new * N_HEADS)[:, None] % N_HEADS) == (jnp.arange(ATT_WIDTH)[None, :] // HEAD_DIM)
    n_new = _round_up(t_new, SUBLANES)
    gf = norm_f_g[None, :]

    xp = x_prompt.reshape(bp * seq, D_MODEL)
    xs = jnp.transpose(x_sample, (1, 0, 2)).reshape(ms, D_MODEL)

    def to_seq_major(a):
        return jnp.transpose(a.reshape(t_new, bd, a.shape[-1]), (1, 0, 2))

    outs = {k: [] for k in ("fp", "cp", "ffp", "fs", "avs", "cs", "ffs")}
    kt_p = jnp.zeros((bp, depth, ATT_WIDTH, seq), F32)
    vt_p = jnp.zeros((bp, depth, ATT_WIDTH, seq), F32)
    kt_s = jnp.zeros((1, depth, ATT_WIDTH, ms), F32)
    vt_s = jnp.zeros((1, depth, ATT_WIDTH, ms), F32)
    for l in range(depth):
        lw = _layer_weights(l, norm1_g, w_in, b_forget, sgu_ln_g, sgu_ln_b, sgu_w, sgu_b, w_proj_a, w_proj_b,
                            conv_w, conv_b, conv_ln_g, conv_ln_b, w_proj_c, w_out, norm2_g, w_up,
                            ffn_conv_w, ffn_conv_b, w_down)
        last = l == depth - 1

        au, av, qx, kx, lf, c, glu, gate, kt_p, vt_p = _inproj(
            xp, lw["g1"], lw["w_in"], lw["bf"], lw["lng"], lw["lnb"], tri_p, kt_p, vt_p, l, nseq=bp, tm=tm_in)
        attn = _attn_prompt(qx.reshape(bp, seq, QKX_WIDTH), kx.reshape(bp, seq, QKX_WIDTH), vt_p, l,
                            tq=tq, pairs=attn_pairs)
        xp, cstate = _mix(au, av, attn.reshape(bp * seq, ATT_WIDTH), glu, gate, xp,
                          jnp.zeros((bp, hr_conv_p, CONV_CH), F32),
                          lw["sgu_w"].astype(BF16), _mix_bias(lw["sgu_b"]),
                          lw["wa"], lw["wb"], lw["cw"], lw["cb"], lw["cg"], lw["cbeta"], lw["wc"], lw["wo"],
                          nseq=bp, tm=tm_mix, rs=1)
        xp, fstate = _ffn(xp, lw["g2"], jnp.zeros((bp, hr_ffn_p, 2 * D_FF), F32), lw["wup"], lw["fcw"],
                          lw["fcb"], lw["wdn"], gf, nseq=bp, tm=tm_ffn, rs=1, final_norm=last)
        outs["fp"].append(lf.reshape(bp, seq, N_HEADS))
        outs["cp"].append(cstate[:, hr_conv_p - (CONV_W - 1):])
        outs["ffp"].append(fstate[:, hr_ffn_p - (FFN_CONV_W - 1):])

        au, av, qx, kx, lf, c, glu, gate, kt_s, vt_s = _inproj(
            xs, lw["g1"], lw["w_in"], lw["bf"], lw["lng"], lw["lnb"], tri_s, kt_s, vt_s, l, nseq=1, tm=ms)
        k, v = kt_s[0, l].T, vt_s[0, l].T
        q = qx.reshape(ms, N_HEADS // 2, PAIR_X)[:, :, :LANES].reshape(ms, ATT_WIDTH)
        q_bd = jnp.where(head_rows, jnp.repeat(to_seq_major(q), N_HEADS, axis=1), jnp.zeros((), BF16))
        pad_new = lambda a: jnp.pad(to_seq_major(a).astype(BF16), ((0, 0), (0, n_new - t_new), (0, 0)))
        cn = jnp.tile(jnp.transpose(to_seq_major(c), (0, 2, 1)), (1, t_new, 1))
        cn = jnp.pad(cn, ((0, 0), (0, 0), (0, n_new - t_new)))
        attn = _attn_sample(page_table, l, q_bd, pad_new(k), pad_new(v), cn,
                            cache_kt, cache_vt, cache_lft, npp=PAGES_PER_STEP)
        attn = jnp.transpose(attn, (1, 0, 2)).reshape(ms, ATT_WIDTH).astype(BF16)
        mixw_s = jnp.stack([jnp.kron(lw["sgu_w"][g, :t_new, :t_new], eye_b) for g in range(SGU_GROUPS)])
        mixb_s = _mix_bias(jnp.repeat(lw["sgu_b"][:, :t_new], bd, axis=1))
        hist_c = jnp.transpose(state_conv[:, l], (1, 0, 2)).reshape(1, hr_conv_s, CONV_CH)
        xs, cstate = _mix(au, av, attn, glu, gate, xs, hist_c, mixw_s.astype(BF16), mixb_s,
                          lw["wa"], lw["wb"], lw["cw"], lw["cb"], lw["cg"], lw["cbeta"], lw["wc"], lw["wo"],
                          nseq=1, tm=ms, rs=bd)
        hist_f = jnp.transpose(state_ffn[:, l], (1, 0, 2)).reshape(1, hr_ffn_s, 2 * D_FF)
        xs, fstate = _ffn(xs, lw["g2"], hist_f, lw["wup"], lw["fcw"], lw["fcb"], lw["wdn"], gf,
                          nseq=1, tm=ms, rs=bd, final_norm=last)
        outs["fs"].append(to_seq_major(lf))
        outs["avs"].append(to_seq_major(av))
        outs["cs"].append(jnp.transpose(cstate.reshape(CONV_W - 1, bd, CONV_CH), (1, 0, 2)))
        outs["ffs"].append(jnp.transpose(fstate.reshape(FFN_CONV_W - 1, bd, 2 * D_FF), (1, 0, 2)))

    y_prompt = xp.reshape(bp, seq, D_MODEL)
    y_sample = to_seq_major(xs)
    st = lambda key: jnp.stack(outs[key], axis=1)

    def heads_last_p(a):
        return jnp.transpose(a.reshape(bp, depth, N_HEADS, HEAD_DIM, seq), (0, 1, 4, 2, 3))

    def heads_last_s(a):
        return jnp.transpose(a.reshape(depth, N_HEADS, HEAD_DIM, t_new, bd), (4, 0, 3, 1, 2))

    return (y_prompt, y_sample, heads_last_p(kt_p), heads_last_p(vt_p), st("fp"), st("cp"), st("ffp"),
            heads_last_s(kt_s), heads_last_s(vt_s), st("fs"), st("avs"), st("cs"), st("ffs"))
```
